```python
import math
import jax, jax.numpy as jnp
from jax import lax
import numpy as np

D_MODEL = 1024
BATCH = 16
SEQ = 2048
DEPTH = 2
DEC_BATCH = 2
DEC_SEQ = 16384
PAST_LEN = 128

N_META = 16
GRID_W = 64
Q_BLOCK = 128
ATTN_WIDTH = D_MODEL // 2
SSM_WIDTH = D_MODEL - ATTN_WIDTH
HEAD_DIM = 64
N_Q_HEADS = ATTN_WIDTH // HEAD_DIM
N_KV_HEADS = 2
Q_PER_KV = N_Q_HEADS // N_KV_HEADS
KV_WIDTH = N_KV_HEADS * HEAD_DIM
ROPE_THETA = 10000.0
SSM_GROUP = 16
N_SSM_GROUPS = SSM_WIDTH // SSM_GROUP
SSM_STATE = 64
DT_MIN = 1e-3
DT_MAX = 1e-1
N_EXPERT_GROUPS = 4
EXPERTS_PER_GROUP = 4
N_EXPERTS = N_EXPERT_GROUPS * EXPERTS_PER_GROUP
EXPERT_TOP_K = 2
EXPERT_FF = 256
IN_WIDTH = ATTN_WIDTH + 2 * KV_WIDTH + SSM_WIDTH
DEEPNORM_ALPHA = (2.0 * DEPTH) ** 0.25
DEEPNORM_BETA = (8.0 * DEPTH) ** -0.25
NORM_EPS = 1e-6

kernel_name = 'hymba_axial_gqa_s5_hmoe_encoder'


def layer_norm(x, g, b):
    xf = x.astype(jnp.float32)
    mu = jnp.mean(xf, axis=-1, keepdims=True)
    var = jnp.mean(jnp.square(xf - mu), axis=-1, keepdims=True)
    return ((xf - mu) * lax.rsqrt(var + NORM_EPS) * g + b).astype(x.dtype)


def rms_norm(x, g):
    xf = x.astype(jnp.float32)
    return (xf * lax.rsqrt(jnp.mean(xf * xf, axis=-1, keepdims=True) + NORM_EPS) * g).astype(x.dtype)


def axial_rope_tables(n_tokens):
    rows = n_tokens // GRID_W
    row = jnp.repeat(jnp.arange(rows, dtype=jnp.float32), GRID_W)
    col = jnp.tile(jnp.arange(GRID_W, dtype=jnp.float32), rows)
    pad = jnp.zeros((N_META,), jnp.float32)
    row = jnp.concatenate([pad, row])
    col = jnp.concatenate([pad, col])
    n_freq = HEAD_DIM // 4
    inv_freq = ROPE_THETA ** (-jnp.arange(n_freq, dtype=jnp.float32) / n_freq)
    ang = jnp.concatenate([row[:, None] * inv_freq, col[:, None] * inv_freq], axis=-1)
    return jnp.cos(ang), jnp.sin(ang)


def apply_rope(x, cos, sin):
    xf = x.astype(jnp.float32).reshape(x.shape[:-1] + (HEAD_DIM // 2, 2))
    a, b = xf[..., 0], xf[..., 1]
    c = cos[None, :, None, :]
    s = sin[None, :, None, :]
    out = jnp.stack([a * c - b * s, a * s + b * c], axis=-1)
    return out.reshape(x.shape).astype(x.dtype)


def attend_block(qb, k, v):
    s = jnp.einsum('bqkrd,btkd->bkrqt', qb, k, preferred_element_type=jnp.float32) * (HEAD_DIM ** -0.5)
    p = jax.nn.softmax(s, axis=-1).astype(v.dtype)
    return jnp.einsum('bkrqt,btkd->bqkrd', p, v)


def gqa_attention(q, k, v):
    bsz, t = q.shape[:2]
    n_real = t - N_META
    q = q.reshape(bsz, t, N_KV_HEADS, Q_PER_KV, HEAD_DIM)
    out_meta = attend_block(q[:, :N_META], k, v)
    n_blocks = n_real // Q_BLOCK
    q_blocks = q[:, N_META:].reshape(bsz, n_blocks, Q_BLOCK, N_KV_HEADS, Q_PER_KV, HEAD_DIM)
    q_blocks = jnp.moveaxis(q_blocks, 1, 0)
    out_real = lax.map(lambda qb: attend_block(qb, k, v), q_blocks)
    out_real = jnp.moveaxis(out_real, 0, 1).reshape(bsz, n_real, ATTN_WIDTH)
    return jnp.concatenate([out_meta.reshape(bsz, N_META, ATTN_WIDTH), out_real], axis=1)


def _ssm_combine(left, right):
    a_l, b_l = left
    a_r, b_r = right
    return a_r * a_l, a_r * b_l + b_r


def s5_branch(u, lam_re, lam_im, log_dt, b_re, b_im, c_re, c_im, d_skip, w_glu, b_glu):
    bsz, t = u.shape[:2]
    f32 = jnp.float32
    uf = u.astype(f32).reshape(bsz, t, N_SSM_GROUPS, SSM_GROUP)
    y = uf * d_skip.astype(f32).reshape(N_SSM_GROUPS, SSM_GROUP)
    uc = uf.astype(jnp.complex64)
    for direction in range(2):
        lam = lax.complex(lam_re[direction].astype(f32), lam_im[direction].astype(f32))
        dt = jnp.exp(log_dt[direction].astype(f32))[:, None]
        lam_bar = jnp.exp(lam * dt)
        b = lax.complex(b_re[direction].astype(f32), b_im[direction].astype(f32))
        b_bar = ((lam_bar - 1.0) / lam)[:, :, None] * b
        bu = jnp.einsum('btgh,gph->btgp', uc, b_bar)
        a = jnp.broadcast_to(lam_bar, bu.shape)
        _, states = lax.associative_scan(_ssm_combine, (a, bu), reverse=(direction == 1), axis=1)
        c = lax.complex(c_re[direction].astype(f32), c_im[direction].astype(f32))
        y = y + jnp.real(jnp.einsum('btgp,ghp->btgh', states, c))
    y = jax.nn.gelu(y.reshape(bsz, t, SSM_WIDTH))
    y = y * jax.nn.sigmoid(y @ w_glu.astype(f32) + b_glu.astype(f32))
    return y.astype(u.dtype)


def hierarchical_moe(x, w_group, b_group, w_router, b_router, w_gate, w_up, w_down):
    bsz, t, d = x.shape
    xt = x.reshape(-1, d)
    n_tok = xt.shape[0]
    group_prob = jax.nn.softmax((xt @ w_group + b_group).astype(jnp.float32), axis=-1)
    g_val, g_idx = lax.top_k(group_prob, 1)
    exp_logits = (xt @ w_router + b_router).astype(jnp.float32).reshape(n_tok, N_EXPERT_GROUPS, EXPERTS_PER_GROUP)
    in_group = jnp.take_along_axis(exp_logits, g_idx[:, :, None], axis=1)[:, 0]
    e_val, e_idx = lax.top_k(in_group, EXPERT_TOP_K)
    e_w = jax.nn.softmax(e_val, axis=-1) * g_val
    expert_id = g_idx * EXPERTS_PER_GROUP + e_idx
    gates = jnp.sum(jax.nn.one_hot(expert_id, N_EXPERTS, dtype=jnp.float32) * e_w[..., None], axis=1)
    h = jax.nn.silu(jnp.einsum('nd,edf->nef', xt, w_gate)) * jnp.einsum('nd,edf->nef', xt, w_up)
    y = jnp.einsum('nef,ne,efd->nd', h, gates.astype(h.dtype), w_down)
    return y.reshape(bsz, t, d)


def encoder_layer(x, cos, sin, p, l):
    bsz, t, _ = x.shape
    proj = x @ p['w_in'][l]
    q = proj[..., :ATTN_WIDTH].reshape(bsz, t, N_Q_HEADS, HEAD_DIM)
    k = proj[..., ATTN_WIDTH:ATTN_WIDTH + KV_WIDTH].reshape(bsz, t, N_KV_HEADS, HEAD_DIM)
    v = proj[..., ATTN_WIDTH + KV_WIDTH:ATTN_WIDTH + 2 * KV_WIDTH].reshape(bsz, t, N_KV_HEADS, HEAD_DIM)
    u = proj[..., ATTN_WIDTH + 2 * KV_WIDTH:]
    q = apply_rope(rms_norm(q, p['q_norm_g'][l]), cos, sin)
    k = apply_rope(rms_norm(k, p['k_norm_g'][l]), cos, sin)
    attn = gqa_attention(q, k, v)
    ssm = s5_branch(u, p['ssm_lambda_re'][l], p['ssm_lambda_im'][l], p['ssm_log_dt'][l],
                    p['ssm_b_re'][l], p['ssm_b_im'][l], p['ssm_c_re'][l], p['ssm_c_im'][l],
                    p['ssm_d'][l], p['w_glu'][l], p['b_glu'][l])
    heads = jnp.concatenate([rms_norm(attn, p['attn_out_g'][l]), rms_norm(ssm, p['ssm_out_g'][l])], axis=-1)
    mixed = heads @ p['w_out'][l]
    x = layer_norm(DEEPNORM_ALPHA * x + mixed, p['ln1_g'][l], p['ln1_b'][l])
    moe = hierarchical_moe(x, p['w_group'][l], p['b_group'][l], p['w_router'][l], p['b_router'][l],
                           p['w_gate'][l], p['w_up'][l], p['w_down'][l])
    return layer_norm(DEEPNORM_ALPHA * x + moe, p['ln2_g'][l], p['ln2_b'][l])


def run_trunk(x, p):
    bsz, n_real, _ = x.shape
    meta = jnp.broadcast_to(p['meta_tokens'][None].astype(x.dtype), (bsz, N_META, D_MODEL))
    h = layer_norm(jnp.concatenate([meta, x], axis=1), p['ln_in_g'], p['ln_in_b'])
    cos, sin = axial_rope_tables(n_real)
    for l in range(DEPTH):
        h = encoder_layer(h, cos, sin, p, l)
    return h[:, N_META:]


def setup_inputs(seed: int = 0) -> dict:
    key = jax.random.key(seed)
    ks = jax.random.split(key, 40)
    f32 = jnp.float32
    nrm = lambda k, s: jax.random.normal(k, s, f32)
    G, P, H = N_SSM_GROUPS, SSM_STATE, SSM_GROUP
    v_scale = jnp.concatenate([jnp.ones((ATTN_WIDTH + KV_WIDTH,), f32),
                               jnp.full((KV_WIDTH,), DEEPNORM_BETA, f32),
                               jnp.ones((SSM_WIDTH,), f32)])
    lam_im0 = jnp.broadcast_to(jnp.pi * jnp.arange(P, dtype=f32), (DEPTH, 2, G, P))
    return {
        'x_prompt': nrm(ks[0], (BATCH, SEQ, D_MODEL)),
        'x_sample': nrm(ks[1], (DEC_BATCH, DEC_SEQ, D_MODEL)),
        'meta_tokens': nrm(ks[2], (N_META, D_MODEL)),
        'ln_in_g': 1.0 + 0.02 * nrm(ks[3], (D_MODEL,)),
        'ln_in_b': 0.02 * nrm(ks[4], (D_MODEL,)),
        'w_in': nrm(ks[5], (DEPTH, D_MODEL, IN_WIDTH)) * (D_MODEL ** -0.5) * v_scale,
        'q_norm_g': 1.0 + 0.02 * nrm(ks[6], (DEPTH, HEAD_DIM)),
        'k_norm_g': 1.0 + 0.02 * nrm(ks[7], (DEPTH, HEAD_DIM)),
        'ssm_lambda_re': -0.5 + 0.01 * nrm(ks[8], (DEPTH, 2, G, P)),
        'ssm_lambda_im': lam_im0 + 0.01 * nrm(ks[9], (DEPTH, 2, G, P)),
        'ssm_log_dt': jax.random.uniform(ks[10], (DEPTH, 2, G), f32, math.log(DT_MIN), math.log(DT_MAX)),
        'ssm_b_re': nrm(ks[11], (DEPTH, 2, G, P, H)) * ((2.0 * H) ** -0.5),
        'ssm_b_im': nrm(ks[12], (DEPTH, 2, G, P, H)) * ((2.0 * H) ** -0.5),
        'ssm_c_re': nrm(ks[13], (DEPTH, 2, G, H, P)) * ((2.0 * P) ** -0.5),
        'ssm_c_im': nrm(ks[14], (DEPTH, 2, G, H, P)) * ((2.0 * P) ** -0.5),
        'ssm_d': nrm(ks[15], (DEPTH, SSM_WIDTH)),
        'w_glu': nrm(ks[16], (DEPTH, SSM_WIDTH, SSM_WIDTH)) * (SSM_WIDTH ** -0.5),
        'b_glu': 0.02 * nrm(ks[17], (DEPTH, SSM_WIDTH)),
        'attn_out_g': 1.0 + 0.02 * nrm(ks[18], (DEPTH, ATTN_WIDTH)),
        'ssm_out_g': 1.0 + 0.02 * nrm(ks[19], (DEPTH, SSM_WIDTH)),
        'w_out': nrm(ks[20], (DEPTH, D_MODEL, D_MODEL)) * (D_MODEL ** -0.5) * DEEPNORM_BETA,
        'ln1_g': 1.0 + 0.02 * nrm(ks[21], (DEPTH, D_MODEL)),
        'ln1_b': 0.02 * nrm(ks[22], (DEPTH, D_MODEL)),
        'w_group': nrm(ks[23], (DEPTH, D_MODEL, N_EXPERT_GROUPS)) * (D_MODEL ** -0.5),
        'b_group': 0.01 * nrm(ks[24], (DEPTH, N_EXPERT_GROUPS)),
        'w_router': nrm(ks[25], (DEPTH, D_MODEL, N_EXPERTS)) * (D_MODEL ** -0.5),
        'b_router': 0.01 * nrm(ks[26], (DEPTH, N_EXPERTS)),
        'w_gate': nrm(ks[27], (DEPTH, N_EXPERTS, D_MODEL, EXPERT_FF)) * (D_MODEL ** -0.5),
        'w_up': nrm(ks[28], (DEPTH, N_EXPERTS, D_MODEL, EXPERT_FF)) * (D_MODEL ** -0.5),
        'w_down': nrm(ks[29], (DEPTH, N_EXPERTS, EXPERT_FF, D_MODEL)) * (EXPERT_FF ** -0.5) * DEEPNORM_BETA,
        'ln2_g': 1.0 + 0.02 * nrm(ks[30], (DEPTH, D_MODEL)),
        'ln2_b': 0.02 * nrm(ks[31], (DEPTH, D_MODEL)),
    }


def reference(x_prompt, x_sample, meta_tokens, ln_in_g, ln_in_b, w_in, q_norm_g, k_norm_g,
              ssm_lambda_re, ssm_lambda_im, ssm_log_dt, ssm_b_re, ssm_b_im, ssm_c_re, ssm_c_im,
              ssm_d, w_glu, b_glu, attn_out_g, ssm_out_g, w_out, ln1_g, ln1_b,
              w_group, b_group, w_router, b_router, w_gate, w_up, w_down, ln2_g, ln2_b):
    params = dict(meta_tokens=meta_tokens, ln_in_g=ln_in_g, ln_in_b=ln_in_b, w_in=w_in,
                  q_norm_g=q_norm_g, k_norm_g=k_norm_g, ssm_lambda_re=ssm_lambda_re,
                  ssm_lambda_im=ssm_lambda_im, ssm_log_dt=ssm_log_dt, ssm_b_re=ssm_b_re,
                  ssm_b_im=ssm_b_im, ssm_c_re=ssm_c_re, ssm_c_im=ssm_c_im, ssm_d=ssm_d,
                  w_glu=w_glu, b_glu=b_glu, attn_out_g=attn_out_g, ssm_out_g=ssm_out_g,
                  w_out=w_out, ln1_g=ln1_g, ln1_b=ln1_b, w_group=w_group, b_group=b_group,
                  w_router=w_router, b_router=b_router, w_gate=w_gate, w_up=w_up,
                  w_down=w_down, ln2_g=ln2_g, ln2_b=ln2_b)
    y_prompt = run_trunk(x_prompt, params)
    y_sample = run_trunk(x_sample, params)
    return (y_prompt, y_sample)
```

```python
import functools

import jax
import jax.numpy as jnp
from jax import lax
from jax.experimental import pallas as pl
from jax.experimental.pallas import tpu as pltpu

D_MODEL = 1024
N_META = 16
GRID_W = 64
ATTN_WIDTH = 512
SSM_WIDTH = 512
HEAD_DIM = 64
N_Q_HEADS = 8
N_KV_HEADS = 2
Q_PER_KV = 4
KV_WIDTH = 128
ROPE_THETA = 10000.0
SSM_GROUP = 16
N_SSM_GROUPS = 32
SSM_STATE = 64
N_EXPERT_GROUPS = 4
EXPERTS_PER_GROUP = 4
N_EXPERTS = 16
EXPERT_FF = 256
DEPTH = 2
DEEPNORM_ALPHA = (2.0 * DEPTH) ** 0.25
NORM_EPS = 1e-6

LANES = 128
PREFIX = 128
N_PAD = PREFIX - N_META
CHUNK = 16
CHUNK_W = CHUNK * SSM_GROUP
STATE_W = 4 * SSM_STATE
N_STATE_ROWS = N_SSM_GROUPS * SSM_STATE
CHUNK_LANES = 256
PAD_CHUNKS = N_PAD // CHUNK
Q_TILE = 128
K_TILE = 512
VMEM_LIMIT = 56 * 1024 * 1024

F32 = jnp.float32
BF16 = jnp.bfloat16


def _cparams(sem):
    return pltpu.CompilerParams(dimension_semantics=sem, vmem_limit_bytes=VMEM_LIMIT)


def _layer_norm(x, g, b):
    mu = jnp.mean(x, axis=-1, keepdims=True)
    xc = x - mu
    var = jnp.mean(xc * xc, axis=-1, keepdims=True)
    return xc * lax.rsqrt(var + NORM_EPS) * g + b


def _swap_pairs(x):
    n = x.shape[-1]
    lane = lax.broadcasted_iota(jnp.int32, x.shape, x.ndim - 1)
    nxt = pltpu.roll(x, n - 1, x.ndim - 1)
    prv = pltpu.roll(x, 1, x.ndim - 1)
    return jnp.where((lane & 1) == 0, nxt, prv)


def _inproj_kernel(pre_ln, tm, *refs):
    if pre_ln:
        (x_ref, cos_ref, sin_ref, lng_ref, lnb_ref, w_ref, gq_ref, gk_ref, bdq_ref, bdk_ref,
         h_ref, hb_ref, q_ref, k_ref, v_ref) = refs
    else:
        (x_ref, cos_ref, sin_ref, w_ref, gq_ref, gk_ref, bdq_ref, bdk_ref,
         hb_ref, q_ref, k_ref, v_ref) = refs
    x = x_ref[0]
    if pre_ln:
        h = _layer_norm(x, lng_ref[...], lnb_ref[...])
        h_ref[0] = h
    else:
        h = x
    hb = h.astype(BF16)
    hb_ref[0] = hb
    proj = jnp.dot(hb, w_ref[...], preferred_element_type=F32)
    q = proj[:, :ATTN_WIDTH]
    k = proj[:, ATTN_WIDTH:ATTN_WIDTH + KV_WIDTH]
    v = proj[:, ATTN_WIDTH + KV_WIDTH:]
    cos = cos_ref[...]
    sin = sin_ref[...]

    q_ms = jnp.dot((q * q).astype(BF16), bdq_ref[...], preferred_element_type=F32)
    qn = q * lax.rsqrt(q_ms + NORM_EPS) * gq_ref[...]
    cos4 = jnp.concatenate([cos] * 4, axis=1)
    sin4 = jnp.concatenate([sin] * 4, axis=1)
    qr = qn * cos4 + _swap_pairs(qn) * sin4
    qr_odd = pltpu.roll(qr, ATTN_WIDTH - HEAD_DIM, 1)
    for hh in range(N_Q_HEADS):
        src = qr if hh % 2 == 0 else qr_odd
        base = (hh // 2) * LANES
        q_ref[0, hh] = src[:, base:base + HEAD_DIM].astype(BF16)

    k_ms = jnp.dot((k * k).astype(BF16), bdk_ref[...], preferred_element_type=F32)
    kn = k * lax.rsqrt(k_ms + NORM_EPS) * gk_ref[...]
    kr = kn * cos + _swap_pairs(kn) * sin
    k_ref[0, 0] = kr[:, :HEAD_DIM].astype(BF16)
    k_ref[0, 1] = pltpu.roll(kr, HEAD_DIM, 1)[:, :HEAD_DIM].astype(BF16)

    pos = pl.program_id(1) * tm + lax.broadcasted_iota(jnp.int32, (tm, 1), 0)
    valid = (pos >= N_PAD).astype(F32)
    lane = lax.broadcasted_iota(jnp.int32, (tm, KV_WIDTH), 1)
    ones_col = (lane == HEAD_DIM).astype(F32)
    v_sh = pltpu.roll(v, HEAD_DIM, 1)
    v_ref[0, 0] = (jnp.where(lane < HEAD_DIM, v, ones_col) * valid).astype(BF16)
    v_ref[0, 1] = (jnp.where(lane < HEAD_DIM, v_sh, ones_col) * valid).astype(BF16)


def _inproj(x, cos_t, sin_t, ln, w_qkv, gq, gk, bdq, bdk, tm):
    bsz, tp, _ = x.shape
    pre_ln = ln is not None
    grid = (bsz, tp // tm)
    row = lambda b, j: (b, j, 0)
    const2 = lambda b, j: (0, 0)
    in_specs = [pl.BlockSpec((1, tm, D_MODEL), row),
                pl.BlockSpec((tm, LANES), lambda b, j: (j, 0)),
                pl.BlockSpec((tm, LANES), lambda b, j: (j, 0))]
    args = [x, cos_t, sin_t]
    if pre_ln:
        in_specs += [pl.BlockSpec((1, D_MODEL), const2)] * 2
        args += [ln[0], ln[1]]
    in_specs += [pl.BlockSpec(w_qkv.shape, const2), pl.BlockSpec(gq.shape, const2),
                 pl.BlockSpec(gk.shape, const2), pl.BlockSpec(bdq.shape, const2),
                 pl.BlockSpec(bdk.shape, const2)]
    args += [w_qkv, gq, gk, bdq, bdk]
    head4 = lambda b, j: (b, 0, j, 0)
    out_shape = [jax.ShapeDtypeStruct((bsz, tp, D_MODEL), BF16),
                 jax.ShapeDtypeStruct((bsz, N_Q_HEADS, tp, HEAD_DIM), BF16),
                 jax.ShapeDtypeStruct((bsz, N_KV_HEADS, tp, HEAD_DIM), BF16),
                 jax.ShapeDtypeStruct((bsz, N_KV_HEADS, tp, LANES), BF16)]
    out_specs = [pl.BlockSpec((1, tm, D_MODEL), row),
                 pl.BlockSpec((1, N_Q_HEADS, tm, HEAD_DIM), head4),
                 pl.BlockSpec((1, N_KV_HEADS, tm, HEAD_DIM), head4),
                 pl.BlockSpec((1, N_KV_HEADS, tm, LANES), head4)]
    if pre_ln:
        out_shape = [jax.ShapeDtypeStruct((bsz, tp, D_MODEL), F32)] + out_shape
        out_specs = [pl.BlockSpec((1, tm, D_MODEL), row)] + out_specs
    outs = pl.pallas_call(
        functools.partial(_inproj_kernel, pre_ln, tm),
        grid=grid, in_specs=in_specs, out_specs=out_specs, out_shape=out_shape,
        compiler_params=_cparams(("parallel", "parallel")),
    )(*args)
    if pre_ln:
        return outs
    return [x] + list(outs)


def _attn_kernel(n_main, q_ref, k_ref, v_ref, o_ref, m_sc, acc_sc):
    tq = q_ref.shape[2]
    rows = Q_PER_KV * tq
    q = q_ref[0].reshape(rows, HEAD_DIM)
    nt = (((1,), (1,)), ((), ()))

    def step(kt, vt):
        s = lax.dot_general(q, kt, nt, preferred_element_type=F32)
        m_prev = m_sc[...]
        m_new = jnp.maximum(m_prev, jnp.max(s, axis=1, keepdims=True))
        alpha = jnp.exp(m_prev - m_new)
        p = jnp.exp(s - m_new)
        acc_sc[...] = alpha * acc_sc[...] + jnp.dot(p.astype(BF16), vt, preferred_element_type=F32)
        m_sc[...] = m_new

    m_sc[...] = jnp.full((rows, 1), -jnp.inf, F32)
    acc_sc[...] = jnp.zeros((rows, LANES), F32)
    step(k_ref[0, 0, :PREFIX, :], v_ref[0, 0, :PREFIX, :])

    def body(i, carry):
        off = pl.multiple_of(PREFIX + i * K_TILE, LANES)
        step(k_ref[0, 0, pl.ds(off, K_TILE), :], v_ref[0, 0, pl.ds(off, K_TILE), :])
        return carry

    lax.fori_loop(0, n_main, body, 0)
    acc = acc_sc[...]
    out = acc[:, :HEAD_DIM] / acc[:, HEAD_DIM:HEAD_DIM + 1]
    o_ref[0] = jnp.concatenate([out[r * tq:(r + 1) * tq] for r in range(Q_PER_KV)], axis=1).astype(BF16)


def _attention(q, k, v):
    bsz, _, tp, _ = q.shape
    n_main = (tp - PREFIX) // K_TILE
    grid = (bsz, N_KV_HEADS, tp // Q_TILE)
    return pl.pallas_call(
        functools.partial(_attn_kernel, n_main),
        grid=grid,
        in_specs=[pl.BlockSpec((1, Q_PER_KV, Q_TILE, HEAD_DIM), lambda b, j, i: (b, j, i, 0)),
                  pl.BlockSpec((1, 1, tp, HEAD_DIM), lambda b, j, i: (b, j, 0, 0)),
                  pl.BlockSpec((1, 1, tp, LANES), lambda b, j, i: (b, j, 0, 0))],
        out_specs=pl.BlockSpec((1, Q_TILE, Q_PER_KV * HEAD_DIM), lambda b, j, i: (b, i, j)),
        out_shape=jax.ShapeDtypeStruct((bsz, tp, ATTN_WIDTH), BF16),
        scratch_shapes=[pltpu.VMEM((Q_PER_KV * Q_TILE, 1), F32),
                        pltpu.VMEM((Q_PER_KV * Q_TILE, LANES), F32)],
        compiler_params=_cparams(("parallel", "parallel", "arbitrary")),
    )(q, k, v)


def _ssm_local_kernel(n_chunks, h_ref, wu_ref, in_ref, rt_ref, sloc_ref):
    wl = rt_ref.shape[2]
    col = pl.program_id(1) * wl + lax.broadcasted_iota(jnp.int32, (1, wl), 1)
    valid = (col >= PAD_CHUNKS) & (col < n_chunks)
    nt = (((1,), (1,)), ((), ()))
    for t in range(CHUNK):
        xt = h_ref[0, :, t * D_MODEL:(t + 1) * D_MODEL]
        slab = lax.dot_general(wu_ref[...], xt, nt, preferred_element_type=F32)
        slab = jnp.where(valid, slab, 0.0).astype(BF16)
        for g in range(N_SSM_GROUPS):
            rt_ref[0, g * CHUNK_W + t * SSM_GROUP:g * CHUNK_W + (t + 1) * SSM_GROUP, :] = (
                slab[g * SSM_GROUP:(g + 1) * SSM_GROUP, :])
    for g in range(N_SSM_GROUPS):
        rg = rt_ref[0, g * CHUNK_W:(g + 1) * CHUNK_W, :]
        sl = jnp.dot(in_ref[g], rg, preferred_element_type=F32)
        for c in range(4):
            sloc_ref[0, c, g * SSM_STATE:(g + 1) * SSM_STATE, :] = sl[c * SSM_STATE:(c + 1) * SSM_STATE, :]


def _ssm_local(hb, wu_t, in_t, ncp):
    bsz, tp, _ = hb.shape
    n_chunks = tp // CHUNK
    hv = hb.reshape(bsz, n_chunks, CHUNK * D_MODEL)
    wl = CHUNK_LANES
    return pl.pallas_call(
        functools.partial(_ssm_local_kernel, n_chunks),
        grid=(bsz, ncp // wl),
        in_specs=[pl.BlockSpec((1, wl, CHUNK * D_MODEL), lambda b, c: (b, c, 0)),
                  pl.BlockSpec(wu_t.shape, lambda b, c: (0, 0)),
                  pl.BlockSpec(in_t.shape, lambda b, c: (0, 0, 0))],
        out_specs=[pl.BlockSpec((1, N_SSM_GROUPS * CHUNK_W, wl), lambda b, c: (b, 0, c)),
                   pl.BlockSpec((1, 4, N_STATE_ROWS, wl), lambda b, c: (b, 0, 0, c))],
        out_shape=[jax.ShapeDtypeStruct((bsz, N_SSM_GROUPS * CHUNK_W, ncp), BF16),
                   jax.ShapeDtypeStruct((bsz, 4, N_STATE_ROWS, ncp), F32)],
        compiler_params=_cparams(("parallel", "parallel")),
    )(hv, wu_t, in_t)


def _ssm_scan_kernel(n_steps, sloc_ref, dec_ref, car_ref):
    ncp = sloc_ref.shape[3]
    lane = lax.broadcasted_iota(jnp.int32, (1, ncp), 1)

    def scan(re, im, c0, reverse):
        for s in range(n_steps):
            d = 1 << s
            if reverse:
                keep = lane < ncp - d
                shift = ncp - d
            else:
                keep = lane >= d
                shift = d
            r_re = jnp.where(keep, pltpu.roll(re, shift, 1), 0.0)
            r_im = jnp.where(keep, pltpu.roll(im, shift, 1), 0.0)
            a_re = dec_ref[c0, :, s:s + 1]
            a_im = dec_ref[c0 + 1, :, s:s + 1]
            re, im = re + (a_re * r_re - a_im * r_im), im + (a_re * r_im + a_im * r_re)
        if reverse:
            keep, shift = lane < ncp - 1, ncp - 1
        else:
            keep, shift = lane >= 1, 1
        car_ref[0, c0] = jnp.where(keep, pltpu.roll(re, shift, 1), 0.0).astype(BF16)
        car_ref[0, c0 + 1] = jnp.where(keep, pltpu.roll(im, shift, 1), 0.0).astype(BF16)

    scan(sloc_ref[0, 0], sloc_ref[0, 1], 0, False)
    scan(sloc_ref[0, 2], sloc_ref[0, 3], 2, True)


def _ssm_scan(sloc, dec, rb):
    bsz, _, _, ncp = sloc.shape
    n_steps = max(1, (ncp - 1).bit_length())
    return pl.pallas_call(
        functools.partial(_ssm_scan_kernel, n_steps),
        grid=(bsz, N_STATE_ROWS // rb),
        in_specs=[pl.BlockSpec((1, 4, rb, ncp), lambda b, r: (b, 0, r, 0)),
                  pl.BlockSpec((4, rb, LANES), lambda b, r: (0, r, 0))],
        out_specs=pl.BlockSpec((1, 4, rb, ncp), lambda b, r: (b, 0, r, 0)),
        out_shape=jax.ShapeDtypeStruct((bsz, 4, N_STATE_ROWS, ncp), BF16),
        compiler_params=_cparams(("parallel", "parallel")),
    )(sloc, dec)


def _ssm_out_kernel(rt_ref, car_ref, m_ref, o_ref, wg_ref, bg_ref, out_ref, y_sc):
    for g in range(N_SSM_GROUPS):
        rg = rt_ref[0, g * CHUNK_W:(g + 1) * CHUNK_W, :]
        sc = jnp.concatenate([car_ref[0, c, g * SSM_STATE:(g + 1) * SSM_STATE, :] for c in range(4)], axis=0)
        y = (jnp.dot(m_ref[g], rg, preferred_element_type=F32)
             + jnp.dot(o_ref[g], sc, preferred_element_type=F32))
        y = jax.nn.gelu(y)
        for t in range(CHUNK):
            y_sc[t * SSM_WIDTH + g * SSM_GROUP:t * SSM_WIDTH + (g + 1) * SSM_GROUP, :] = (
                y[t * SSM_GROUP:(t + 1) * SSM_GROUP, :])
    for t in range(CHUNK):
        slab = y_sc[t * SSM_WIDTH:(t + 1) * SSM_WIDTH, :]
        z = jnp.dot(wg_ref[...], slab.astype(BF16), preferred_element_type=F32) + bg_ref[...]
        o = slab * jax.nn.sigmoid(z)
        out_ref[0, :, t * SSM_WIDTH:(t + 1) * SSM_WIDTH] = o.T.astype(BF16)


def _ssm_out(rt, car, m_t, o_t, wglu_t, bglu, tp):
    bsz, _, ncp = rt.shape
    n_chunks = tp // CHUNK
    wl = CHUNK_LANES
    out = pl.pallas_call(
        _ssm_out_kernel,
        grid=(bsz, ncp // wl),
        in_specs=[pl.BlockSpec((1, N_SSM_GROUPS * CHUNK_W, wl), lambda b, c: (b, 0, c)),
                  pl.BlockSpec((1, 4, N_STATE_ROWS, wl), lambda b, c: (b, 0, 0, c)),
                  pl.BlockSpec(m_t.shape, lambda b, c: (0, 0, 0)),
                  pl.BlockSpec(o_t.shape, lambda b, c: (0, 0, 0)),
                  pl.BlockSpec(wglu_t.shape, lambda b, c: (0, 0)),
                  pl.BlockSpec(bglu.shape, lambda b, c: (0, 0))],
        out_specs=pl.BlockSpec((1, wl, CHUNK * SSM_WIDTH), lambda b, c: (b, c, 0)),
        out_shape=jax.ShapeDtypeStruct((bsz, n_chunks, CHUNK * SSM_WIDTH), BF16),
        scratch_shapes=[pltpu.VMEM((CHUNK * SSM_WIDTH, wl), F32)],
        compiler_params=_cparams(("parallel", "parallel")),
    )(rt, car, m_t, o_t, wglu_t, bglu)
    return out.reshape(bsz, tp, SSM_WIDTH)


def _ssm_operators(lam_re, lam_im, log_dt, b_re, b_im, c_re, c_im, d_skip):
    g_n, p_n, h_n = N_SSM_GROUPS, SSM_STATE, SSM_GROUP
    pows, bbar, cmat = [], [], []
    for direction in range(2):
        lam = lax.complex(lam_re[direction].astype(F32), lam_im[direction].astype(F32))
        dt = jnp.exp(log_dt[direction].astype(F32))[:, None]
        ldt = lam * dt
        lam_bar = jnp.exp(ldt)
        b = lax.complex(b_re[direction].astype(F32), b_im[direction].astype(F32))
        bbar.append(((lam_bar - 1.0) / lam)[:, :, None] * b)
        cmat.append(lax.complex(c_re[direction].astype(F32), c_im[direction].astype(F32)))
        j = jnp.arange(CHUNK + 1, dtype=F32)[:, None, None].astype(jnp.complex64)
        pows.append(jnp.exp(j * ldt[None]))
    kern = [jnp.real(jnp.einsum('ghp,jgp,gpk->jghk', cmat[d], pows[d][:CHUNK], bbar[d])) for d in range(2)]
    t_idx = jnp.arange(CHUNK)
    diff = t_idx[:, None] - t_idx[None, :]
    kf = jnp.where((diff >= 0)[:, :, None, None, None], kern[0][jnp.clip(diff, 0, CHUNK - 1)], 0.0)
    kb = jnp.where((diff <= 0)[:, :, None, None, None], kern[1][jnp.clip(-diff, 0, CHUNK - 1)], 0.0)
    skip = (jnp.eye(CHUNK, dtype=F32)[:, :, None, None, None]
            * (jnp.eye(h_n, dtype=F32)[None] * d_skip.astype(F32).reshape(g_n, 1, h_n))[None, None])
    m_t = (kf + kb + skip).transpose(2, 0, 3, 1, 4).reshape(g_n, CHUNK_W, CHUNK_W)
    in_f = jnp.einsum('sgp,gph->gpsh', pows[0][:CHUNK][::-1], bbar[0]).reshape(g_n, p_n, CHUNK_W)
    in_b = jnp.einsum('sgp,gph->gpsh', pows[1][:CHUNK], bbar[1]).reshape(g_n, p_n, CHUNK_W)
    in_t = jnp.concatenate([jnp.real(in_f), jnp.imag(in_f), jnp.real(in_b), jnp.imag(in_b)], axis=1)
    out_f = jnp.einsum('ghp,tgp->gthp', cmat[0], pows[0][1:]).reshape(g_n, CHUNK_W, p_n)
    out_b = jnp.einsum('ghp,tgp->gthp', cmat[1], pows[1][1:][::-1]).reshape(g_n, CHUNK_W, p_n)
    o_t = jnp.concatenate([jnp.real(out_f), -jnp.imag(out_f), jnp.real(out_b), -jnp.imag(out_b)], axis=2)
    steps = (CHUNK * (2.0 ** jnp.arange(LANES, dtype=F32)))
    dec = []
    for direction in range(2):
        lam = lax.complex(lam_re[direction].astype(F32), lam_im[direction].astype(F32)).reshape(-1)
        dt = jnp.repeat(jnp.exp(log_dt[direction].astype(F32)), p_n)
        ldt = lam * dt
        re = jnp.exp(jnp.real(ldt)[:, None] * steps[None])
        ang = jnp.imag(ldt)[:, None] * steps[None]
        dec += [re * jnp.cos(ang), re * jnp.sin(ang)]
    dec = jnp.stack(dec)
    return m_t.astype(BF16), in_t.astype(BF16), o_t.astype(BF16), dec


def _outproj_kernel(attn_ref, ssm_ref, h_ref, wo_ref, ga_ref, gs_ref, lg_ref, lb_ref,
                    wr_hi_ref, wr_lo_ref, br_ref, x1_ref, gate_ref):
    a = attn_ref[0].astype(F32)
    s = ssm_ref[0].astype(F32)
    an = a * lax.rsqrt(jnp.mean(a * a, axis=-1, keepdims=True) + NORM_EPS) * ga_ref[...]
    sn = s * lax.rsqrt(jnp.mean(s * s, axis=-1, keepdims=True) + NORM_EPS) * gs_ref[...]
    mixed = (jnp.dot(an.astype(BF16), wo_ref[:ATTN_WIDTH, :], preferred_element_type=F32)
             + jnp.dot(sn.astype(BF16), wo_ref[ATTN_WIDTH:, :], preferred_element_type=F32))
    x1 = _layer_norm(DEEPNORM_ALPHA * h_ref[0] + mixed, lg_ref[...], lb_ref[...])
    x1_ref[0] = x1

    x_hi = x1.astype(BF16)
    x_lo = (x1 - x_hi.astype(F32)).astype(BF16)
    logits = (jnp.dot(x_hi, wr_hi_ref[...], preferred_element_type=F32)
              + jnp.dot(x_lo, wr_hi_ref[...], preferred_element_type=F32)
              + jnp.dot(x_hi, wr_lo_ref[...], preferred_element_type=F32)) + br_ref[...]
    lane = lax.broadcasted_iota(jnp.int32, logits.shape, 1)
    neg = -jnp.inf
    is_group = (lane >= N_EXPERTS) & (lane < N_EXPERTS + N_EXPERT_GROUPS)
    glog = jnp.where(is_group, logits, neg)
    gmax = jnp.max(glog, axis=1, keepdims=True)
    g_val = 1.0 / jnp.sum(jnp.exp(glog - gmax), axis=1, keepdims=True)
    g_idx = jnp.min(jnp.where(glog == gmax, lane, LANES), axis=1, keepdims=True) - N_EXPERTS
    in_group = (lane < N_EXPERTS) & ((lane >> 2) == g_idx)
    el = jnp.where(in_group, logits, neg)
    v1 = jnp.max(el, axis=1, keepdims=True)
    i1 = jnp.min(jnp.where(el == v1, lane, LANES), axis=1, keepdims=True)
    el2 = jnp.where(lane == i1, neg, el)
    v2 = jnp.max(el2, axis=1, keepdims=True)
    i2 = jnp.min(jnp.where(el2 == v2, lane, LANES), axis=1, keepdims=True)
    e2 = jnp.exp(v2 - v1)
    w1 = g_val / (1.0 + e2)
    gate_ref[0] = jnp.where(lane == i1, w1, jnp.where(lane == i2, w1 * e2, 0.0))


def _outproj(attn, ssm, h, wo, ga, gs, lg, lb, wr_hi, wr_lo, br, tm):
    bsz, tp, _ = h.shape
    row = lambda b, j: (b, j, 0)
    c2 = lambda b, j: (0, 0)
    return pl.pallas_call(
        _outproj_kernel,
        grid=(bsz, tp // tm),
        in_specs=[pl.BlockSpec((1, tm, ATTN_WIDTH), row), pl.BlockSpec((1, tm, SSM_WIDTH), row),
                  pl.BlockSpec((1, tm, D_MODEL), row), pl.BlockSpec(wo.shape, c2),
                  pl.BlockSpec(ga.shape, c2), pl.BlockSpec(gs.shape, c2),
                  pl.BlockSpec(lg.shape, c2), pl.BlockSpec(lb.shape, c2),
                  pl.BlockSpec(wr_hi.shape, c2), pl.BlockSpec(wr_lo.shape, c2), pl.BlockSpec(br.shape, c2)],
        out_specs=[pl.BlockSpec((1, tm, D_MODEL), row), pl.BlockSpec((1, tm, LANES), row)],
        out_shape=[jax.ShapeDtypeStruct((bsz, tp, D_MODEL), F32),
                   jax.ShapeDtypeStruct((bsz, tp, LANES), F32)],
        compiler_params=_cparams(("parallel", "parallel")),
    )(attn, ssm, h, wo, ga, gs, lg, lb, wr_hi, wr_lo, br)


def _moe_kernel(x_ref, gate_ref, wgu_ref, wd_ref, lg_ref, lb_ref, y_ref, xb_sc, acc_sc):
    e = pl.program_id(2)

    @pl.when(e == 0)
    def _():
        xb_sc[...] = x_ref[0].astype(BF16)
        acc_sc[...] = jnp.zeros_like(acc_sc)

    hgu = jnp.dot(xb_sc[...], wgu_ref[0], preferred_element_type=F32)
    gates = gate_ref[0]
    lane = lax.broadcasted_iota(jnp.int32, gates.shape, 1)
    gate_e = jnp.sum(jnp.where(lane == e, gates, 0.0), axis=1, keepdims=True)
    hh = jax.nn.silu(hgu[:, :EXPERT_FF]) * hgu[:, EXPERT_FF:] * gate_e
    acc_sc[...] += jnp.dot(hh.astype(BF16), wd_ref[0], preferred_element_type=F32)

    @pl.when(e == N_EXPERTS - 1)
    def _():
        y_ref[0] = _layer_norm(DEEPNORM_ALPHA * x_ref[0] + acc_sc[...], lg_ref[...], lb_ref[...])


def _moe(x1, gates, wgu, wd, lg, lb, tm):
    bsz, tp, _ = x1.shape
    row = lambda b, j, e: (b, j, 0)
    c2 = lambda b, j, e: (0, 0)
    return pl.pallas_call(
        _moe_kernel,
        grid=(bsz, tp // tm, N_EXPERTS),
        in_specs=[pl.BlockSpec((1, tm, D_MODEL), row), pl.BlockSpec((1, tm, LANES), row),
                  pl.BlockSpec((1, D_MODEL, 2 * EXPERT_FF), lambda b, j, e: (e, 0, 0)),
                  pl.BlockSpec((1, EXPERT_FF, D_MODEL), lambda b, j, e: (e, 0, 0)),
                  pl.BlockSpec(lg.shape, c2), pl.BlockSpec(lb.shape, c2)],
        out_specs=pl.BlockSpec((1, tm, D_MODEL), row),
        out_shape=jax.ShapeDtypeStruct((bsz, tp, D_MODEL), F32),
        scratch_shapes=[pltpu.VMEM((tm, D_MODEL), BF16), pltpu.VMEM((tm, D_MODEL), F32)],
        compiler_params=_cparams(("parallel", "parallel", "arbitrary")),
    )(x1, gates, wgu, wd, lg, lb)


def _rope_tables(n_real):
    r = jnp.arange(n_real, dtype=jnp.int32)
    row = jnp.concatenate([jnp.zeros((PREFIX,), F32), (r // GRID_W).astype(F32)])
    col = jnp.concatenate([jnp.zeros((PREFIX,), F32), (r % GRID_W).astype(F32)])
    n_freq = HEAD_DIM // 4
    inv_freq = ROPE_THETA ** (-jnp.arange(n_freq, dtype=F32) / n_freq)
    ang = jnp.concatenate([row[:, None] * inv_freq, col[:, None] * inv_freq], axis=-1)
    cos = jnp.repeat(jnp.cos(ang), 2, axis=-1)
    sin = jnp.repeat(jnp.sin(ang), 2, axis=-1) * jnp.tile(jnp.array([-1.0, 1.0], F32), HEAD_DIM // 2)
    return jnp.tile(cos, (1, 2)), jnp.tile(sin, (1, 2))


def _row_tile(tp, cap):
    best = 8
    for t in range(8, cap + 1, 8):
        if tp % t == 0:
            best = t
    return best


def _layer_weights(p, l):
    w_in = p['w_in'][l]
    kv_end = ATTN_WIDTH + 2 * KV_WIDTH
    m_t, in_t, o_t, dec = _ssm_operators(p['ssm_lambda_re'][l], p['ssm_lambda_im'][l], p['ssm_log_dt'][l],
                                         p['ssm_b_re'][l], p['ssm_b_im'][l], p['ssm_c_re'][l],
                                         p['ssm_c_im'][l], p['ssm_d'][l])
    head_avg = jnp.kron(jnp.eye(N_Q_HEADS, dtype=F32), jnp.full((HEAD_DIM, HEAD_DIM), 1.0 / HEAD_DIM, F32))
    wr = jnp.zeros((D_MODEL, LANES), F32)
    wr = wr.at[:, :N_EXPERTS].set(p['w_router'][l]).at[:, N_EXPERTS:N_EXPERTS + N_EXPERT_GROUPS].set(p['w_group'][l])
    br = jnp.zeros((1, LANES), F32)
    br = br.at[0, :N_EXPERTS].set(p['b_router'][l]).at[0, N_EXPERTS:N_EXPERTS + N_EXPERT_GROUPS].set(p['b_group'][l])
    wr_hi = wr.astype(BF16)
    return dict(
        w_qkv=w_in[:, :kv_end].astype(BF16),
        gq=(jnp.tile(p['q_norm_g'][l], N_Q_HEADS) * (HEAD_DIM ** -0.5)).reshape(1, ATTN_WIDTH),
        gk=jnp.tile(p['k_norm_g'][l], N_KV_HEADS).reshape(1, KV_WIDTH),
        bdq=head_avg.astype(BF16), bdk=head_avg[:KV_WIDTH, :KV_WIDTH].astype(BF16),
        wu_t=w_in[:, kv_end:].T.astype(BF16),
        m_t=m_t, in_t=in_t, o_t=o_t, dec=dec,
        wglu_t=p['w_glu'][l].T.astype(BF16), bglu=p['b_glu'][l].reshape(SSM_WIDTH, 1).astype(F32),
        wo=p['w_out'][l].astype(BF16),
        ga=p['attn_out_g'][l].reshape(1, -1), gs=p['ssm_out_g'][l].reshape(1, -1),
        ln1g=p['ln1_g'][l].reshape(1, -1), ln1b=p['ln1_b'][l].reshape(1, -1),
        wr_hi=wr_hi, wr_lo=(wr - wr_hi.astype(F32)).astype(BF16), br=br,
        wgu=jnp.concatenate([p['w_gate'][l], p['w_up'][l]], axis=-1).astype(BF16),
        wd=p['w_down'][l].astype(BF16),
        ln2g=p['ln2_g'][l].reshape(1, -1), ln2b=p['ln2_b'][l].reshape(1, -1),
    )


def _trunk(x, p, weights):
    bsz, n_real, _ = x.shape
    tp = n_real + PREFIX
    meta = jnp.broadcast_to(p['meta_tokens'][None].astype(x.dtype), (bsz, N_META, D_MODEL))
    h = jnp.concatenate([jnp.zeros((bsz, N_PAD, D_MODEL), x.dtype), meta, x], axis=1)
    cos_t, sin_t = _rope_tables(n_real)
    n_chunks = tp // CHUNK
    ncp = -(-n_chunks // CHUNK_LANES) * CHUNK_LANES
    tm = _row_tile(tp, 1408)
    ln = (p['ln_in_g'].reshape(1, -1), p['ln_in_b'].reshape(1, -1))
    for l in range(DEPTH):
        w = weights[l]
        h, hb, q, k, v = _inproj(h, cos_t, sin_t, ln if l == 0 else None, w['w_qkv'], w['gq'], w['gk'],
                                 w['bdq'], w['bdk'], tm)
        attn = _attention(q, k, v)
        rt, sloc = _ssm_local(hb, w['wu_t'], w['in_t'], ncp)
        car = _ssm_scan(sloc, w['dec'], 256)
        ssm = _ssm_out(rt, car, w['m_t'], w['o_t'], w['wglu_t'], w['bglu'], tp)
        x1, gates = _outproj(attn, ssm, h, w['wo'], w['ga'], w['gs'], w['ln1g'], w['ln1b'],
                             w['wr_hi'], w['wr_lo'], w['br'], tm)
        h = _moe(x1, gates, w['wgu'], w['wd'], w['ln2g'], w['ln2b'], tm)
    return h[:, PREFIX:]


def kernel(x_prompt, x_sample, meta_tokens, ln_in_g, ln_in_b, w_in, q_norm_g, k_norm_g, ssm_lambda_re, ssm_lambda_im, ssm_log_dt, ssm_b_re, ssm_b_im, ssm_c_re, ssm_c_im, ssm_d, w_glu, b_glu, attn_out_g, ssm_out_g, w_out, ln1_g, ln1_b, w_group, b_group, w_router, b_router, w_gate, w_up, w_down, ln2_g, ln2_b):
    p = dict(meta_tokens=meta_tokens, ln_in_g=ln_in_g, ln_in_b=ln_in_b, w_in=w_in,
             q_norm_g=q_norm_g, k_norm_g=k_norm_g, ssm_lambda_re=ssm_lambda_re,
             ssm_lambda_im=ssm_lambda_im, ssm_log_dt=ssm_log_dt, ssm_b_re=ssm_b_re,
             ssm_b_im=ssm_b_im, ssm_c_re=ssm_c_re, ssm_c_im=ssm_c_im, ssm_d=ssm_d,
             w_glu=w_glu, b_glu=b_glu, attn_out_g=attn_out_g, ssm_out_g=ssm_out_g,
             w_out=w_out, ln1_g=ln1_g, ln1_b=ln1_b, w_group=w_group, b_group=b_group,
             w_router=w_router, b_router=b_router, w_gate=w_gate, w_up=w_up,
             w_down=w_down, ln2_g=ln2_g, ln2_b=ln2_b)
    weights = [_layer_weights(p, l) for l in range(DEPTH)]
    return (_trunk(x_prompt, p, weights), _trunk(x_sample, p, weights))
```

```python
import functools

import jax
import jax.numpy as jnp
from jax import lax
from jax.experimental import pallas as pl
from jax.experimental.pallas import tpu as pltpu

D_MODEL = 1024
N_META = 16
GRID_W = 64
ATTN_WIDTH = 512
SSM_WIDTH = 512
HEAD_DIM = 64
N_Q_HEADS = 8
N_KV_HEADS = 2
Q_PER_KV = 4
KV_WIDTH = 128
ROPE_THETA = 10000.0
SSM_GROUP = 16
N_SSM_GROUPS = 32
SSM_STATE = 64
N_EXPERT_GROUPS = 4
EXPERTS_PER_GROUP = 4
N_EXPERTS = 16
EXPERT_FF = 256
DEPTH = 2
DEEPNORM_ALPHA = (2.0 * DEPTH) ** 0.25
NORM_EPS = 1e-6

LANES = 128
PREFIX = 128
N_PAD = PREFIX - N_META
CHUNK = 16
CHUNK_W = CHUNK * SSM_GROUP
STATE_W = 4 * SSM_STATE
N_STATE_ROWS = N_SSM_GROUPS * SSM_STATE
CHUNK_LANES = 256
PAD_CHUNKS = N_PAD // CHUNK
Q_TILE = 128
K_TILE = 2048
VMEM_LIMIT = 56 * 1024 * 1024

F32 = jnp.float32
BF16 = jnp.bfloat16


def _cparams(sem):
    return pltpu.CompilerParams(dimension_semantics=sem, vmem_limit_bytes=VMEM_LIMIT)


def _layer_norm(x, g, b):
    mu = jnp.mean(x, axis=-1, keepdims=True)
    xc = x - mu
    var = jnp.mean(xc * xc, axis=-1, keepdims=True)
    return xc * lax.rsqrt(var + NORM_EPS) * g + b


def _swap_pairs(x):
    n = x.shape[-1]
    lane = lax.broadcasted_iota(jnp.int32, x.shape, x.ndim - 1)
    nxt = pltpu.roll(x, n - 1, x.ndim - 1)
    prv = pltpu.roll(x, 1, x.ndim - 1)
    return jnp.where((lane & 1) == 0, nxt, prv)


def _inproj_kernel(pre_ln, tm, *refs):
    if pre_ln:
        (x_ref, cos_ref, sin_ref, lng_ref, lnb_ref, w_ref, gq_ref, gk_ref, bdq_ref, bdk_ref,
         h_ref, hb_ref, q_ref, k_ref, v_ref) = refs
    else:
        (x_ref, cos_ref, sin_ref, w_ref, gq_ref, gk_ref, bdq_ref, bdk_ref,
         hb_ref, q_ref, k_ref, v_ref) = refs
    x = x_ref[0]
    if pre_ln:
        h = _layer_norm(x, lng_ref[...], lnb_ref[...])
        h_ref[0] = h
    else:
        h = x
    hb = h.astype(BF16)
    hb_ref[0] = hb
    proj = jnp.dot(hb, w_ref[...], preferred_element_type=F32)
    q = proj[:, :ATTN_WIDTH]
    k = proj[:, ATTN_WIDTH:ATTN_WIDTH + KV_WIDTH]
    v = proj[:, ATTN_WIDTH + KV_WIDTH:]
    cos = cos_ref[...]
    sin = sin_ref[...]

    q_ms = jnp.dot((q * q).astype(BF16), bdq_ref[...], preferred_element_type=F32)
    qn = q * lax.rsqrt(q_ms + NORM_EPS) * gq_ref[...]
    cos4 = jnp.concatenate([cos] * 4, axis=1)
    sin4 = jnp.concatenate([sin] * 4, axis=1)
    qr = qn * cos4 + _swap_pairs(qn) * sin4
    qr_odd = pltpu.roll(qr, ATTN_WIDTH - HEAD_DIM, 1)
    for hh in range(N_Q_HEADS):
        src = qr if hh % 2 == 0 else qr_odd
        base = (hh // 2) * LANES
        q_ref[0, hh] = src[:, base:base + HEAD_DIM].astype(BF16)

    k_ms = jnp.dot((k * k).astype(BF16), bdk_ref[...], preferred_element_type=F32)
    kn = k * lax.rsqrt(k_ms + NORM_EPS) * gk_ref[...]
    kr = kn * cos + _swap_pairs(kn) * sin
    k_ref[0, 0] = kr[:, :HEAD_DIM].astype(BF16)
    k_ref[0, 1] = pltpu.roll(kr, HEAD_DIM, 1)[:, :HEAD_DIM].astype(BF16)

    pos = pl.program_id(1) * tm + lax.broadcasted_iota(jnp.int32, (tm, 1), 0)
    valid = (pos >= N_PAD).astype(F32)
    lane = lax.broadcasted_iota(jnp.int32, (tm, KV_WIDTH), 1)
    ones_col = (lane == HEAD_DIM).astype(F32)
    v_sh = pltpu.roll(v, HEAD_DIM, 1)
    v_ref[0, 0] = (jnp.where(lane < HEAD_DIM, v, ones_col) * valid).astype(BF16)
    v_ref[0, 1] = (jnp.where(lane < HEAD_DIM, v_sh, ones_col) * valid).astype(BF16)


def _inproj(x, cos_t, sin_t, ln, w_qkv, gq, gk, bdq, bdk, tm):
    bsz, tp, _ = x.shape
    pre_ln = ln is not None
    grid = (bsz, tp // tm)
    row = lambda b, j: (b, j, 0)
    const2 = lambda b, j: (0, 0)
    in_specs = [pl.BlockSpec((1, tm, D_MODEL), row),
                pl.BlockSpec((tm, LANES), lambda b, j: (j, 0)),
                pl.BlockSpec((tm, LANES), lambda b, j: (j, 0))]
    args = [x, cos_t, sin_t]
    if pre_ln:
        in_specs += [pl.BlockSpec((1, D_MODEL), const2)] * 2
        args += [ln[0], ln[1]]
    in_specs += [pl.BlockSpec(w_qkv.shape, const2), pl.BlockSpec(gq.shape, const2),
                 pl.BlockSpec(gk.shape, const2), pl.BlockSpec(bdq.shape, const2),
                 pl.BlockSpec(bdk.shape, const2)]
    args += [w_qkv, gq, gk, bdq, bdk]
    head4 = lambda b, j: (b, 0, j, 0)
    out_shape = [jax.ShapeDtypeStruct((bsz, tp, D_MODEL), BF16),
                 jax.ShapeDtypeStruct((bsz, N_Q_HEADS, tp, HEAD_DIM), BF16),
                 jax.ShapeDtypeStruct((bsz, N_KV_HEADS, tp, HEAD_DIM), BF16),
                 jax.ShapeDtypeStruct((bsz, N_KV_HEADS, tp, LANES), BF16)]
    out_specs = [pl.BlockSpec((1, tm, D_MODEL), row),
                 pl.BlockSpec((1, N_Q_HEADS, tm, HEAD_DIM), head4),
                 pl.BlockSpec((1, N_KV_HEADS, tm, HEAD_DIM), head4),
                 pl.BlockSpec((1, N_KV_HEADS, tm, LANES), head4)]
    if pre_ln:
        out_shape = [jax.ShapeDtypeStruct((bsz, tp, D_MODEL), F32)] + out_shape
        out_specs = [pl.BlockSpec((1, tm, D_MODEL), row)] + out_specs
    outs = pl.pallas_call(
        functools.partial(_inproj_kernel, pre_ln, tm),
        grid=grid, in_specs=in_specs, out_specs=out_specs, out_shape=out_shape,
        compiler_params=_cparams(("parallel", "parallel")),
    )(*args)
    if pre_ln:
        return outs
    return [x] + list(outs)


def _attn_kernel(n_steps, q_ref, k_ref, v_ref, o_ref, m_sc, acc_sc):
    tq = q_ref.shape[2]
    rows = Q_PER_KV * tq
    q = q_ref[0].reshape(rows, HEAD_DIM)
    nt = (((1,), (1,)), ((), ()))

    def step(kt, vt, first):
        s = lax.dot_general(q, kt, nt, preferred_element_type=F32)
        m_cur = jnp.broadcast_to(jnp.max(s, axis=1, keepdims=True), (rows, LANES))
        m_new = m_cur if first else jnp.maximum(m_sc[...], m_cur)
        p = jnp.exp(s - jnp.tile(m_new, (1, s.shape[1] // LANES)))
        pv = jnp.dot(p.astype(BF16), vt, preferred_element_type=F32)
        if first:
            acc_sc[...] = pv
        else:
            acc_sc[...] = jnp.exp(m_sc[...] - m_new) * acc_sc[...] + pv
        m_sc[...] = m_new

    step(k_ref[0, 0, :PREFIX, :], v_ref[0, 0, :PREFIX, :], True)

    def body(i, carry):
        off = pl.multiple_of(PREFIX + i * K_TILE, LANES)
        step(k_ref[0, 0, pl.ds(off, K_TILE), :], v_ref[0, 0, pl.ds(off, K_TILE), :], False)
        return carry

    lax.fori_loop(0, n_steps, body, 0, unroll=4 if n_steps % 4 == 0 else 1)
    acc = acc_sc[...]
    out = acc[:, :HEAD_DIM] / acc[:, HEAD_DIM:HEAD_DIM + 1]
    o_ref[0] = jnp.concatenate([out[r * tq:(r + 1) * tq] for r in range(Q_PER_KV)], axis=1).astype(BF16)


def _attention(q, k, v):
    bsz, _, tp, _ = q.shape
    n_steps = (tp - PREFIX) // K_TILE
    grid = (bsz, N_KV_HEADS, tp // Q_TILE)
    return pl.pallas_call(
        functools.partial(_attn_kernel, n_steps),
        grid=grid,
        in_specs=[pl.BlockSpec((1, Q_PER_KV, Q_TILE, HEAD_DIM), lambda b, j, i: (b, j, i, 0)),
                  pl.BlockSpec((1, 1, tp, HEAD_DIM), lambda b, j, i: (b, j, 0, 0)),
                  pl.BlockSpec((1, 1, tp, LANES), lambda b, j, i: (b, j, 0, 0))],
        out_specs=pl.BlockSpec((1, Q_TILE, Q_PER_KV * HEAD_DIM), lambda b, j, i: (b, i, j)),
        out_shape=jax.ShapeDtypeStruct((bsz, tp, ATTN_WIDTH), BF16),
        scratch_shapes=[pltpu.VMEM((Q_PER_KV * Q_TILE, LANES), F32),
                        pltpu.VMEM((Q_PER_KV * Q_TILE, LANES), F32)],
        compiler_params=_cparams(("parallel", "parallel", "arbitrary")),
    )(q, k, v)


def _ssm_local_kernel(n_chunks, h_ref, wu_ref, in_ref, rt_ref, sloc_ref):
    wl = rt_ref.shape[2]
    col = pl.program_id(1) * wl + lax.broadcasted_iota(jnp.int32, (1, wl), 1)
    valid = (col >= PAD_CHUNKS) & (col < n_chunks)
    nt = (((1,), (1,)), ((), ()))
    for t in range(CHUNK):
        xt = h_ref[0, :, t * D_MODEL:(t + 1) * D_MODEL]
        slab = lax.dot_general(wu_ref[...], xt, nt, preferred_element_type=F32)
        slab = jnp.where(valid, slab, 0.0).astype(BF16)
        for g in range(N_SSM_GROUPS):
            rt_ref[0, g * CHUNK_W + t * SSM_GROUP:g * CHUNK_W + (t + 1) * SSM_GROUP, :] = (
                slab[g * SSM_GROUP:(g + 1) * SSM_GROUP, :])
    for g in range(N_SSM_GROUPS):
        rg = rt_ref[0, g * CHUNK_W:(g + 1) * CHUNK_W, :]
        sl = jnp.dot(in_ref[g], rg, preferred_element_type=F32)
        for c in range(4):
            sloc_ref[0, c, g * SSM_STATE:(g + 1) * SSM_STATE, :] = sl[c * SSM_STATE:(c + 1) * SSM_STATE, :]


def _ssm_local(hb, wu_t, in_t, ncp):
    bsz, tp, _ = hb.shape
    n_chunks = tp // CHUNK
    hv = hb.reshape(bsz, n_chunks, CHUNK * D_MODEL)
    wl = CHUNK_LANES
    return pl.pallas_call(
        functools.partial(_ssm_local_kernel, n_chunks),
        grid=(bsz, ncp // wl),
        in_specs=[pl.BlockSpec((1, wl, CHUNK * D_MODEL), lambda b, c: (b, c, 0)),
                  pl.BlockSpec(wu_t.shape, lambda b, c: (0, 0)),
                  pl.BlockSpec(in_t.shape, lambda b, c: (0, 0, 0))],
        out_specs=[pl.BlockSpec((1, N_SSM_GROUPS * CHUNK_W, wl), lambda b, c: (b, 0, c)),
                   pl.BlockSpec((1, 4, N_STATE_ROWS, wl), lambda b, c: (b, 0, 0, c))],
        out_shape=[jax.ShapeDtypeStruct((bsz, N_SSM_GROUPS * CHUNK_W, ncp), BF16),
                   jax.ShapeDtypeStruct((bsz, 4, N_STATE_ROWS, ncp), F32)],
        compiler_params=_cparams(("parallel", "parallel")),
    )(hv, wu_t, in_t)


def _ssm_scan_kernel(n_steps, sloc_ref, dec_ref, car_ref):
    ncp = sloc_ref.shape[3]
    lane = lax.broadcasted_iota(jnp.int32, (1, ncp), 1)

    def scan(re, im, c0, reverse):
        for s in range(n_steps):
            d = 1 << s
            if reverse:
                keep = lane < ncp - d
                shift = ncp - d
            else:
                keep = lane >= d
                shift = d
            r_re = jnp.where(keep, pltpu.roll(re, shift, 1), 0.0)
            r_im = jnp.where(keep, pltpu.roll(im, shift, 1), 0.0)
            a_re = dec_ref[c0, :, s:s + 1]
            a_im = dec_ref[c0 + 1, :, s:s + 1]
            re, im = re + (a_re * r_re - a_im * r_im), im + (a_re * r_im + a_im * r_re)
        if reverse:
            keep, shift = lane < ncp - 1, ncp - 1
        else:
            keep, shift = lane >= 1, 1
        car_ref[0, c0] = jnp.where(keep, pltpu.roll(re, shift, 1), 0.0).astype(BF16)
        car_ref[0, c0 + 1] = jnp.where(keep, pltpu.roll(im, shift, 1), 0.0).astype(BF16)

    scan(sloc_ref[0, 0], sloc_ref[0, 1], 0, False)
    scan(sloc_ref[0, 2], sloc_ref[0, 3], 2, True)


def _ssm_scan(sloc, dec, rb):
    bsz, _, _, ncp = sloc.shape
    n_steps = max(1, (ncp - 1).bit_length())
    return pl.pallas_call(
        functools.partial(_ssm_scan_kernel, n_steps),
        grid=(bsz, N_STATE_ROWS // rb),
        in_specs=[pl.BlockSpec((1, 4, rb, ncp), lambda b, r: (b, 0, r, 0)),
                  pl.BlockSpec((4, rb, LANES), lambda b, r: (0, r, 0))],
        out_specs=pl.BlockSpec((1, 4, rb, ncp), lambda b, r: (b, 0, r, 0)),
        out_shape=jax.ShapeDtypeStruct((bsz, 4, N_STATE_ROWS, ncp), BF16),
        compiler_params=_cparams(("parallel", "parallel")),
    )(sloc, dec)


def _ssm_out_kernel(rt_ref, car_ref, m_ref, o_ref, wg_ref, bg_ref, out_ref, y_sc):
    for g in range(N_SSM_GROUPS):
        rg = rt_ref[0, g * CHUNK_W:(g + 1) * CHUNK_W, :]
        sc = jnp.concatenate([car_ref[0, c, g * SSM_STATE:(g + 1) * SSM_STATE, :] for c in range(4)], axis=0)
        y = (jnp.dot(m_ref[g], rg, preferred_element_type=F32)
             + jnp.dot(o_ref[g], sc, preferred_element_type=F32))
        y = jax.nn.gelu(y)
        for t in range(CHUNK):
            y_sc[t * SSM_WIDTH + g * SSM_GROUP:t * SSM_WIDTH + (g + 1) * SSM_GROUP, :] = (
                y[t * SSM_GROUP:(t + 1) * SSM_GROUP, :])
    for t in range(CHUNK):
        slab = y_sc[t * SSM_WIDTH:(t + 1) * SSM_WIDTH, :]
        z = jnp.dot(wg_ref[...], slab.astype(BF16), preferred_element_type=F32) + bg_ref[...]
        o = slab * jax.nn.sigmoid(z)
        out_ref[0, :, t * SSM_WIDTH:(t + 1) * SSM_WIDTH] = o.T.astype(BF16)


def _ssm_out(rt, car, m_t, o_t, wglu_t, bglu, tp):
    bsz, _, ncp = rt.shape
    n_chunks = tp // CHUNK
    wl = CHUNK_LANES
    out = pl.pallas_call(
        _ssm_out_kernel,
        grid=(bsz, ncp // wl),
        in_specs=[pl.BlockSpec((1, N_SSM_GROUPS * CHUNK_W, wl), lambda b, c: (b, 0, c)),
                  pl.BlockSpec((1, 4, N_STATE_ROWS, wl), lambda b, c: (b, 0, 0, c)),
                  pl.BlockSpec(m_t.shape, lambda b, c: (0, 0, 0)),
                  pl.BlockSpec(o_t.shape, lambda b, c: (0, 0, 0)),
                  pl.BlockSpec(wglu_t.shape, lambda b, c: (0, 0)),
                  pl.BlockSpec(bglu.shape, lambda b, c: (0, 0))],
        out_specs=pl.BlockSpec((1, wl, CHUNK * SSM_WIDTH), lambda b, c: (b, c, 0)),
        out_shape=jax.ShapeDtypeStruct((bsz, n_chunks, CHUNK * SSM_WIDTH), BF16),
        scratch_shapes=[pltpu.VMEM((CHUNK * SSM_WIDTH, wl), F32)],
        compiler_params=_cparams(("parallel", "parallel")),
    )(rt, car, m_t, o_t, wglu_t, bglu)
    return out.reshape(bsz, tp, SSM_WIDTH)


def _ssm_operators(lam_re, lam_im, log_dt, b_re, b_im, c_re, c_im, d_skip):
    g_n, p_n, h_n = N_SSM_GROUPS, SSM_STATE, SSM_GROUP
    hi = lax.Precision.HIGHEST
    kern, w_in, w_out, dec = [], [], [], []
    scan_steps = jnp.arange(LANES)
    scan_mult = jnp.where(scan_steps < 24, CHUNK * 2.0 ** jnp.minimum(scan_steps, 24).astype(F32), 0.0)
    for direction in range(2):
        l_re, l_im = lam_re[direction].astype(F32), lam_im[direction].astype(F32)
        dt = jnp.exp(log_dt[direction].astype(F32))[:, None]
        ldt_re, ldt_im = l_re * dt, l_im * dt
        j = jnp.arange(CHUNK + 1, dtype=F32)[:, None, None]
        mag = jnp.exp(j * ldt_re[None])
        p_re, p_im = mag * jnp.cos(j * ldt_im[None]), mag * jnp.sin(j * ldt_im[None])
        a, b = p_re[1] - 1.0, p_im[1]
        den = l_re * l_re + l_im * l_im
        z_re, z_im = (a * l_re + b * l_im) / den, (b * l_re - a * l_im) / den
        br, bi = b_re[direction].astype(F32), b_im[direction].astype(F32)
        bb_re = z_re[:, :, None] * br - z_im[:, :, None] * bi
        bb_im = z_re[:, :, None] * bi + z_im[:, :, None] * br
        w_re = p_re[:, :, :, None] * bb_re[None] - p_im[:, :, :, None] * bb_im[None]
        w_im = p_re[:, :, :, None] * bb_im[None] + p_im[:, :, :, None] * bb_re[None]
        cr, ci = c_re[direction].astype(F32), c_im[direction].astype(F32)
        kern.append(jnp.einsum('ghp,jgpk->jghk', cr, w_re[:CHUNK], precision=hi)
                    - jnp.einsum('ghp,jgpk->jghk', ci, w_im[:CHUNK], precision=hi))
        w_in.append((w_re, w_im))
        w_out.append((cr[None] * p_re[:, :, None, :] - ci[None] * p_im[:, :, None, :],
                      cr[None] * p_im[:, :, None, :] + ci[None] * p_re[:, :, None, :]))
        d_mag = jnp.exp(ldt_re.reshape(-1, 1) * scan_mult[None])
        d_ang = ldt_im.reshape(-1, 1) * scan_mult[None]
        dec += [d_mag * jnp.cos(d_ang), d_mag * jnp.sin(d_ang)]
    t_idx = jnp.arange(CHUNK)
    diff = t_idx[:, None] - t_idx[None, :]
    kf = jnp.where((diff >= 0)[:, :, None, None, None], kern[0][jnp.clip(diff, 0, CHUNK - 1)], 0.0)
    kb = jnp.where((diff <= 0)[:, :, None, None, None], kern[1][jnp.clip(-diff, 0, CHUNK - 1)], 0.0)
    skip = (jnp.eye(CHUNK, dtype=F32)[:, :, None, None, None]
            * (jnp.eye(h_n, dtype=F32)[None] * d_skip.astype(F32).reshape(g_n, 1, h_n))[None, None])
    m_t = (kf + kb + skip).transpose(2, 0, 3, 1, 4).reshape(g_n, CHUNK_W, CHUNK_W)
    in_parts = []
    for direction in range(2):
        for part in w_in[direction]:
            sel = part[:CHUNK][::-1] if direction == 0 else part[:CHUNK]
            in_parts.append(sel.transpose(1, 2, 0, 3).reshape(g_n, p_n, CHUNK_W))
    in_t = jnp.concatenate(in_parts, axis=1)
    out_parts = []
    for direction in range(2):
        for sign, part in zip((1.0, -1.0), w_out[direction]):
            sel = part[1:] if direction == 0 else part[1:][::-1]
            out_parts.append(sign * sel.transpose(1, 0, 2, 3).reshape(g_n, CHUNK_W, p_n))
    o_t = jnp.concatenate(out_parts, axis=2)
    dec = jnp.stack(dec)
    return m_t.astype(BF16), in_t.astype(BF16), o_t.astype(BF16), dec


def _outproj_kernel(attn_ref, ssm_ref, h_ref, wo_ref, ga_ref, gs_ref, lg_ref, lb_ref,
                    wr_hi_ref, wr_lo_ref, br_ref, x1_ref, gate_ref):
    a = attn_ref[0].astype(F32)
    s = ssm_ref[0].astype(F32)
    an = a * lax.rsqrt(jnp.mean(a * a, axis=-1, keepdims=True) + NORM_EPS) * ga_ref[...]
    sn = s * lax.rsqrt(jnp.mean(s * s, axis=-1, keepdims=True) + NORM_EPS) * gs_ref[...]
    mixed = (jnp.dot(an.astype(BF16), wo_ref[:ATTN_WIDTH, :], preferred_element_type=F32)
             + jnp.dot(sn.astype(BF16), wo_ref[ATTN_WIDTH:, :], preferred_element_type=F32))
    x1 = _layer_norm(DEEPNORM_ALPHA * h_ref[0] + mixed, lg_ref[...], lb_ref[...])
    x1_ref[0] = x1

    x_hi = x1.astype(BF16)
    x_lo = (x1 - x_hi.astype(F32)).astype(BF16)
    logits = (jnp.dot(x_hi, wr_hi_ref[...], preferred_element_type=F32)
              + jnp.dot(x_lo, wr_hi_ref[...], preferred_element_type=F32)
              + jnp.dot(x_hi, wr_lo_ref[...], preferred_element_type=F32)) + br_ref[...]
    lane = lax.broadcasted_iota(jnp.int32, logits.shape, 1)
    neg = -jnp.inf
    is_group = (lane >= N_EXPERTS) & (lane < N_EXPERTS + N_EXPERT_GROUPS)
    glog = jnp.where(is_group, logits, neg)
    gmax = jnp.max(glog, axis=1, keepdims=True)
    g_val = 1.0 / jnp.sum(jnp.exp(glog - gmax), axis=1, keepdims=True)
    g_idx = jnp.min(jnp.where(glog == gmax, lane, LANES), axis=1, keepdims=True) - N_EXPERTS
    in_group = (lane < N_EXPERTS) & ((lane >> 2) == g_idx)
    el = jnp.where(in_group, logits, neg)
    v1 = jnp.max(el, axis=1, keepdims=True)
    i1 = jnp.min(jnp.where(el == v1, lane, LANES), axis=1, keepdims=True)
    el2 = jnp.where(lane == i1, neg, el)
    v2 = jnp.max(el2, axis=1, keepdims=True)
    i2 = jnp.min(jnp.where(el2 == v2, lane, LANES), axis=1, keepdims=True)
    e2 = jnp.exp(v2 - v1)
    w1 = g_val / (1.0 + e2)
    gate_ref[0] = jnp.where(lane == i1, w1, jnp.where(lane == i2, w1 * e2, 0.0))


def _outproj(attn, ssm, h, wo, ga, gs, lg, lb, wr_hi, wr_lo, br, tm):
    bsz, tp, _ = h.shape
    row = lambda b, j: (b, j, 0)
    c2 = lambda b, j: (0, 0)
    return pl.pallas_call(
        _outproj_kernel,
        grid=(bsz, tp // tm),
        in_specs=[pl.BlockSpec((1, tm, ATTN_WIDTH), row), pl.BlockSpec((1, tm, SSM_WIDTH), row),
                  pl.BlockSpec((1, tm, D_MODEL), row), pl.BlockSpec(wo.shape, c2),
                  pl.BlockSpec(ga.shape, c2), pl.BlockSpec(gs.shape, c2),
                  pl.BlockSpec(lg.shape, c2), pl.BlockSpec(lb.shape, c2),
                  pl.BlockSpec(wr_hi.shape, c2), pl.BlockSpec(wr_lo.shape, c2), pl.BlockSpec(br.shape, c2)],
        out_specs=[pl.BlockSpec((1, tm, D_MODEL), row), pl.BlockSpec((1, tm, LANES), row)],
        out_shape=[jax.ShapeDtypeStruct((bsz, tp, D_MODEL), F32),
                   jax.ShapeDtypeStruct((bsz, tp, LANES), F32)],
        compiler_params=_cparams(("parallel", "parallel")),
    )(attn, ssm, h, wo, ga, gs, lg, lb, wr_hi, wr_lo, br)


def _moe_kernel(x_ref, gate_ref, wgu_ref, wd_ref, lg_ref, lb_ref, y_ref, xb_sc, acc_sc):
    e = pl.program_id(2)

    @pl.when(e == 0)
    def _():
        xb_sc[...] = x_ref[0].astype(BF16)
        acc_sc[...] = jnp.zeros_like(acc_sc)

    hgu = jnp.dot(xb_sc[...], wgu_ref[0], preferred_element_type=F32)
    gates = gate_ref[0]
    lane = lax.broadcasted_iota(jnp.int32, gates.shape, 1)
    gate_e = jnp.sum(jnp.where(lane == e, gates, 0.0), axis=1, keepdims=True)
    hh = jax.nn.silu(hgu[:, :EXPERT_FF]) * hgu[:, EXPERT_FF:] * gate_e
    acc_sc[...] += jnp.dot(hh.astype(BF16), wd_ref[0], preferred_element_type=F32)

    @pl.when(e == N_EXPERTS - 1)
    def _():
        y_ref[0] = _layer_norm(DEEPNORM_ALPHA * x_ref[0] + acc_sc[...], lg_ref[...], lb_ref[...])


def _moe(x1, gates, wgu, wd, lg, lb, tm):
    bsz, tp, _ = x1.shape
    row = lambda b, j, e: (b, j, 0)
    c2 = lambda b, j, e: (0, 0)
    return pl.pallas_call(
        _moe_kernel,
        grid=(bsz, tp // tm, N_EXPERTS),
        in_specs=[pl.BlockSpec((1, tm, D_MODEL), row), pl.BlockSpec((1, tm, LANES), row),
                  pl.BlockSpec((1, D_MODEL, 2 * EXPERT_FF), lambda b, j, e: (e, 0, 0)),
                  pl.BlockSpec((1, EXPERT_FF, D_MODEL), lambda b, j, e: (e, 0, 0)),
                  pl.BlockSpec(lg.shape, c2), pl.BlockSpec(lb.shape, c2)],
        out_specs=pl.BlockSpec((1, tm, D_MODEL), row),
        out_shape=jax.ShapeDtypeStruct((bsz, tp, D_MODEL), F32),
        scratch_shapes=[pltpu.VMEM((tm, D_MODEL), BF16), pltpu.VMEM((tm, D_MODEL), F32)],
        compiler_params=_cparams(("parallel", "parallel", "arbitrary")),
    )(x1, gates, wgu, wd, lg, lb)


def _rope_tables(n_real):
    r = jnp.arange(n_real, dtype=jnp.int32)
    row = jnp.concatenate([jnp.zeros((PREFIX,), F32), (r // GRID_W).astype(F32)])
    col = jnp.concatenate([jnp.zeros((PREFIX,), F32), (r % GRID_W).astype(F32)])
    n_freq = HEAD_DIM // 4
    inv_freq = ROPE_THETA ** (-jnp.arange(n_freq, dtype=F32) / n_freq)
    ang = jnp.concatenate([row[:, None] * inv_freq, col[:, None] * inv_freq], axis=-1)
    cos = jnp.repeat(jnp.cos(ang), 2, axis=-1)
    sin = jnp.repeat(jnp.sin(ang), 2, axis=-1) * jnp.tile(jnp.array([-1.0, 1.0], F32), HEAD_DIM // 2)
    return jnp.tile(cos, (1, 2)), jnp.tile(sin, (1, 2))


def _row_tile(tp, cap):
    best = 8
    for t in range(8, cap + 1, 8):
        if tp % t == 0:
            best = t
    return best


def _layer_weights(p, l):
    w_in = p['w_in'][l]
    kv_end = ATTN_WIDTH + 2 * KV_WIDTH
    m_t, in_t, o_t, dec = _ssm_operators(p['ssm_lambda_re'][l], p['ssm_lambda_im'][l], p['ssm_log_dt'][l],
                                         p['ssm_b_re'][l], p['ssm_b_im'][l], p['ssm_c_re'][l],
                                         p['ssm_c_im'][l], p['ssm_d'][l])
    head_avg = jnp.kron(jnp.eye(N_Q_HEADS, dtype=F32), jnp.full((HEAD_DIM, HEAD_DIM), 1.0 / HEAD_DIM, F32))
    wr = jnp.zeros((D_MODEL, LANES), F32)
    wr = wr.at[:, :N_EXPERTS].set(p['w_router'][l]).at[:, N_EXPERTS:N_EXPERTS + N_EXPERT_GROUPS].set(p['w_group'][l])
    br = jnp.zeros((1, LANES), F32)
    br = br.at[0, :N_EXPERTS].set(p['b_router'][l]).at[0, N_EXPERTS:N_EXPERTS + N_EXPERT_GROUPS].set(p['b_group'][l])
    wr_hi = wr.astype(BF16)
    return dict(
        w_qkv=w_in[:, :kv_end].astype(BF16),
        gq=(jnp.tile(p['q_norm_g'][l], N_Q_HEADS) * (HEAD_DIM ** -0.5)).reshape(1, ATTN_WIDTH),
        gk=jnp.tile(p['k_norm_g'][l], N_KV_HEADS).reshape(1, KV_WIDTH),
        bdq=head_avg.astype(BF16), bdk=head_avg[:KV_WIDTH, :KV_WIDTH].astype(BF16),
        wu_t=w_in[:, kv_end:].T.astype(BF16),
        m_t=m_t, in_t=in_t, o_t=o_t, dec=dec,
        wglu_t=p['w_glu'][l].T.astype(BF16), bglu=p['b_glu'][l].reshape(SSM_WIDTH, 1).astype(F32),
        wo=p['w_out'][l].astype(BF16),
        ga=p['attn_out_g'][l].reshape(1, -1), gs=p['ssm_out_g'][l].reshape(1, -1),
        ln1g=p['ln1_g'][l].reshape(1, -1), ln1b=p['ln1_b'][l].reshape(1, -1),
        wr_hi=wr_hi, wr_lo=(wr - wr_hi.astype(F32)).astype(BF16), br=br,
        wgu=jnp.concatenate([p['w_gate'][l], p['w_up'][l]], axis=-1).astype(BF16),
        wd=p['w_down'][l].astype(BF16),
        ln2g=p['ln2_g'][l].reshape(1, -1), ln2b=p['ln2_b'][l].reshape(1, -1),
    )


def _trunk(x, p, weights):
    bsz, n_real, _ = x.shape
    tp = n_real + PREFIX
    meta = jnp.broadcast_to(p['meta_tokens'][None].astype(x.dtype), (bsz, N_META, D_MODEL))
    h = jnp.concatenate([jnp.zeros((bsz, N_PAD, D_MODEL), x.dtype), meta, x], axis=1)
    cos_t, sin_t = _rope_tables(n_real)
    n_chunks = tp // CHUNK
    ncp = -(-n_chunks // CHUNK_LANES) * CHUNK_LANES
    tm = _row_tile(tp, 1408)
    ln = (p['ln_in_g'].reshape(1, -1), p['ln_in_b'].reshape(1, -1))
    for l in range(DEPTH):
        w = weights[l]
        h, hb, q, k, v = _inproj(h, cos_t, sin_t, ln if l == 0 else None, w['w_qkv'], w['gq'], w['gk'],
                                 w['bdq'], w['bdk'], tm)
        attn = _attention(q, k, v)
        rt, sloc = _ssm_local(hb, w['wu_t'], w['in_t'], ncp)
        car = _ssm_scan(sloc, w['dec'], 256)
        ssm = _ssm_out(rt, car, w['m_t'], w['o_t'], w['wglu_t'], w['bglu'], tp)
        x1, gates = _outproj(attn, ssm, h, w['wo'], w['ga'], w['gs'], w['ln1g'], w['ln1b'],
                             w['wr_hi'], w['wr_lo'], w['br'], tm)
        h = _moe(x1, gates, w['wgu'], w['wd'], w['ln2g'], w['ln2b'], tm)
    return h[:, PREFIX:]


def kernel(x_prompt, x_sample, meta_tokens, ln_in_g, ln_in_b, w_in, q_norm_g, k_norm_g, ssm_lambda_re, ssm_lambda_im, ssm_log_dt, ssm_b_re, ssm_b_im, ssm_c_re, ssm_c_im, ssm_d, w_glu, b_glu, attn_out_g, ssm_out_g, w_out, ln1_g, ln1_b, w_group, b_group, w_router, b_router, w_gate, w_up, w_down, ln2_g, ln2_b):
    p = dict(meta_tokens=meta_tokens, ln_in_g=ln_in_g, ln_in_b=ln_in_b, w_in=w_in,
             q_norm_g=q_norm_g, k_norm_g=k_norm_g, ssm_lambda_re=ssm_lambda_re,
             ssm_lambda_im=ssm_lambda_im, ssm_log_dt=ssm_log_dt, ssm_b_re=ssm_b_re,
             ssm_b_im=ssm_b_im, ssm_c_re=ssm_c_re, ssm_c_im=ssm_c_im, ssm_d=ssm_d,
             w_glu=w_glu, b_glu=b_glu, attn_out_g=attn_out_g, ssm_out_g=ssm_out_g,
             w_out=w_out, ln1_g=ln1_g, ln1_b=ln1_b, w_group=w_group, b_group=b_group,
             w_router=w_router, b_router=b_router, w_gate=w_gate, w_up=w_up,
             w_down=w_down, ln2_g=ln2_g, ln2_b=ln2_b)
    weights = [_layer_weights(p, l) for l in range(DEPTH)]
    return (_trunk(x_prompt, p, weights), _trunk(x_sample, p, weights))
```

```python
import functools

import jax
import jax.numpy as jnp
from jax import lax
from jax.experimental import pallas as pl
from jax.experimental.pallas import tpu as pltpu

D_MODEL = 1024
N_META = 16
GRID_W = 64
ATTN_WIDTH = 512
SSM_WIDTH = 512
HEAD_DIM = 64
N_Q_HEADS = 8
N_KV_HEADS = 2
Q_PER_KV = 4
KV_WIDTH = 128
ROPE_THETA = 10000.0
SSM_GROUP = 16
N_SSM_GROUPS = 32
SSM_STATE = 64
N_EXPERT_GROUPS = 4
EXPERTS_PER_GROUP = 4
N_EXPERTS = 16
EXPERT_FF = 256
DEPTH = 2
DEEPNORM_ALPHA = (2.0 * DEPTH) ** 0.25
NORM_EPS = 1e-6

LANES = 128
SUBLANES = 8
PREFIX = 128
N_PAD = PREFIX - N_META
CHUNK = 16
CHUNK_W = CHUNK * SSM_GROUP
STATE_W = 4 * SSM_STATE
N_STATE_ROWS = N_SSM_GROUPS * SSM_STATE
CHUNK_LANES = 256
PAD_CHUNKS = N_PAD // CHUNK
SCAN_LANES = 512
Q_TILE = 128
K_TILE = 2048
VMEM_LIMIT = 56 * 1024 * 1024
SSM_OUT_VMEM_LIMIT = 60 * 1024 * 1024

F32 = jnp.float32
BF16 = jnp.bfloat16


def _cparams(sem, vmem_limit=VMEM_LIMIT):
    return pltpu.CompilerParams(dimension_semantics=sem, vmem_limit_bytes=vmem_limit)


def _layer_norm(x, g, b):
    mu = jnp.mean(x, axis=-1, keepdims=True)
    xc = x - mu
    var = jnp.mean(xc * xc, axis=-1, keepdims=True)
    return xc * lax.rsqrt(var + NORM_EPS) * g + b


def _swap_pairs(x):
    n = x.shape[-1]
    lane = lax.broadcasted_iota(jnp.int32, x.shape, x.ndim - 1)
    nxt = pltpu.roll(x, n - 1, x.ndim - 1)
    prv = pltpu.roll(x, 1, x.ndim - 1)
    return jnp.where((lane & 1) == 0, nxt, prv)


def _inproj_kernel(pre_ln, tm, *refs):
    if pre_ln:
        (x_ref, cos_ref, sin_ref, lng_ref, lnb_ref, w_ref, gq_ref, gk_ref, bdq_ref, bdk_ref,
         h_ref, hb_ref, q_ref, k_ref, v_ref) = refs
    else:
        (x_ref, cos_ref, sin_ref, w_ref, gq_ref, gk_ref, bdq_ref, bdk_ref,
         hb_ref, q_ref, k_ref, v_ref) = refs
    x = x_ref[0]
    if pre_ln:
        h = _layer_norm(x, lng_ref[...], lnb_ref[...])
        h_ref[0] = h
    else:
        h = x
    hb = h.astype(BF16)
    hb_ref[0] = hb
    proj = jnp.dot(hb, w_ref[...], preferred_element_type=F32)
    q = proj[:, :ATTN_WIDTH]
    k = proj[:, ATTN_WIDTH:ATTN_WIDTH + KV_WIDTH]
    v = proj[:, ATTN_WIDTH + KV_WIDTH:]
    cos = cos_ref[...]
    sin = sin_ref[...]

    q_ms = jnp.dot((q * q).astype(BF16), bdq_ref[...], preferred_element_type=F32)
    qn = q * lax.rsqrt(q_ms + NORM_EPS) * gq_ref[...]
    cos4 = jnp.concatenate([cos] * 4, axis=1)
    sin4 = jnp.concatenate([sin] * 4, axis=1)
    qr = qn * cos4 + _swap_pairs(qn) * sin4
    qr_odd = pltpu.roll(qr, ATTN_WIDTH - HEAD_DIM, 1)
    for hh in range(N_Q_HEADS):
        src = qr if hh % 2 == 0 else qr_odd
        base = (hh // 2) * LANES
        q_ref[0, hh] = src[:, base:base + HEAD_DIM].astype(BF16)

    k_ms = jnp.dot((k * k).astype(BF16), bdk_ref[...], preferred_element_type=F32)
    kn = k * lax.rsqrt(k_ms + NORM_EPS) * gk_ref[...]
    kr = kn * cos + _swap_pairs(kn) * sin
    k_ref[0, 0] = kr[:, :HEAD_DIM].astype(BF16)
    k_ref[0, 1] = pltpu.roll(kr, HEAD_DIM, 1)[:, :HEAD_DIM].astype(BF16)

    pos = pl.program_id(1) * tm + lax.broadcasted_iota(jnp.int32, (tm, 1), 0)
    valid = (pos >= N_PAD).astype(F32)
    lane = lax.broadcasted_iota(jnp.int32, (tm, KV_WIDTH), 1)
    ones_col = (lane == HEAD_DIM).astype(F32)
    v_sh = pltpu.roll(v, HEAD_DIM, 1)
    v_ref[0, 0] = (jnp.where(lane < HEAD_DIM, v, ones_col) * valid).astype(BF16)
    v_ref[0, 1] = (jnp.where(lane < HEAD_DIM, v_sh, ones_col) * valid).astype(BF16)


def _inproj(x, cos_t, sin_t, ln, w_qkv, gq, gk, bdq, bdk, tm):
    bsz, tp, _ = x.shape
    pre_ln = ln is not None
    grid = (bsz, tp // tm)
    row = lambda b, j: (b, j, 0)
    const2 = lambda b, j: (0, 0)
    in_specs = [pl.BlockSpec((1, tm, D_MODEL), row),
                pl.BlockSpec((tm, LANES), lambda b, j: (j, 0)),
                pl.BlockSpec((tm, LANES), lambda b, j: (j, 0))]
    args = [x, cos_t, sin_t]
    if pre_ln:
        in_specs += [pl.BlockSpec((1, D_MODEL), const2)] * 2
        args += [ln[0], ln[1]]
    in_specs += [pl.BlockSpec(w_qkv.shape, const2), pl.BlockSpec(gq.shape, const2),
                 pl.BlockSpec(gk.shape, const2), pl.BlockSpec(bdq.shape, const2),
                 pl.BlockSpec(bdk.shape, const2)]
    args += [w_qkv, gq, gk, bdq, bdk]
    head4 = lambda b, j: (b, 0, j, 0)
    out_shape = [jax.ShapeDtypeStruct((bsz, tp, D_MODEL), BF16),
                 jax.ShapeDtypeStruct((bsz, N_Q_HEADS, tp, HEAD_DIM), BF16),
                 jax.ShapeDtypeStruct((bsz, N_KV_HEADS, tp, HEAD_DIM), BF16),
                 jax.ShapeDtypeStruct((bsz, N_KV_HEADS, tp, LANES), BF16)]
    out_specs = [pl.BlockSpec((1, tm, D_MODEL), row),
                 pl.BlockSpec((1, N_Q_HEADS, tm, HEAD_DIM), head4),
                 pl.BlockSpec((1, N_KV_HEADS, tm, HEAD_DIM), head4),
                 pl.BlockSpec((1, N_KV_HEADS, tm, LANES), head4)]
    if pre_ln:
        out_shape = [jax.ShapeDtypeStruct((bsz, tp, D_MODEL), F32)] + out_shape
        out_specs = [pl.BlockSpec((1, tm, D_MODEL), row)] + out_specs
    outs = pl.pallas_call(
        functools.partial(_inproj_kernel, pre_ln, tm),
        grid=grid, in_specs=in_specs, out_specs=out_specs, out_shape=out_shape,
        compiler_params=_cparams(("parallel", "parallel")),
    )(*args)
    if pre_ln:
        return outs
    return [x] + list(outs)


def _attn_kernel(n_steps, q_ref, k_ref, v_ref, o_ref, m_sc, acc_sc):
    tq = q_ref.shape[2]
    rows = Q_PER_KV * tq
    q = q_ref[0].reshape(rows, HEAD_DIM)
    nt = (((1,), (1,)), ((), ()))

    def step(kt, vt, first):
        s = lax.dot_general(q, kt, nt, preferred_element_type=F32)
        m_cur = jnp.broadcast_to(jnp.max(s, axis=1, keepdims=True), (rows, LANES))
        m_new = m_cur if first else jnp.maximum(m_sc[...], m_cur)
        p = jnp.exp(s - jnp.tile(m_new, (1, s.shape[1] // LANES)))
        pv = jnp.dot(p.astype(BF16), vt, preferred_element_type=F32)
        if first:
            acc_sc[...] = pv
        else:
            acc_sc[...] = jnp.exp(m_sc[...] - m_new) * acc_sc[...] + pv
        m_sc[...] = m_new

    step(k_ref[0, 0, :PREFIX, :], v_ref[0, 0, :PREFIX, :], True)

    def body(i, carry):
        off = pl.multiple_of(PREFIX + i * K_TILE, LANES)
        step(k_ref[0, 0, pl.ds(off, K_TILE), :], v_ref[0, 0, pl.ds(off, K_TILE), :], False)
        return carry

    lax.fori_loop(0, n_steps, body, 0, unroll=4 if n_steps % 4 == 0 else 1)
    acc = acc_sc[...]
    out = acc[:, :HEAD_DIM] / acc[:, HEAD_DIM:HEAD_DIM + 1]
    o_ref[0] = jnp.concatenate([out[r * tq:(r + 1) * tq] for r in range(Q_PER_KV)], axis=1).astype(BF16)


def _attention(q, k, v):
    bsz, _, tp, _ = q.shape
    n_steps = (tp - PREFIX) // K_TILE
    grid = (bsz, N_KV_HEADS, tp // Q_TILE)
    return pl.pallas_call(
        functools.partial(_attn_kernel, n_steps),
        grid=grid,
        in_specs=[pl.BlockSpec((1, Q_PER_KV, Q_TILE, HEAD_DIM), lambda b, j, i: (b, j, i, 0)),
                  pl.BlockSpec((1, 1, tp, HEAD_DIM), lambda b, j, i: (b, j, 0, 0)),
                  pl.BlockSpec((1, 1, tp, LANES), lambda b, j, i: (b, j, 0, 0))],
        out_specs=pl.BlockSpec((1, Q_TILE, Q_PER_KV * HEAD_DIM), lambda b, j, i: (b, i, j)),
        out_shape=jax.ShapeDtypeStruct((bsz, tp, ATTN_WIDTH), BF16),
        scratch_shapes=[pltpu.VMEM((Q_PER_KV * Q_TILE, LANES), F32),
                        pltpu.VMEM((Q_PER_KV * Q_TILE, LANES), F32)],
        compiler_params=_cparams(("parallel", "parallel", "arbitrary")),
    )(q, k, v)


def _ssm_local_kernel(n_chunks, h_ref, wu_ref, in_ref, rt_ref, sloc_ref):
    wl = rt_ref.shape[2]
    col = pl.program_id(1) * wl + lax.broadcasted_iota(jnp.int32, (1, wl), 1)
    valid = (col >= PAD_CHUNKS) & (col < n_chunks)
    nt = (((1,), (1,)), ((), ()))
    for t in range(CHUNK):
        xt = h_ref[0, :, t * D_MODEL:(t + 1) * D_MODEL]
        slab = lax.dot_general(wu_ref[...], xt, nt, preferred_element_type=F32)
        slab = jnp.where(valid, slab, 0.0).astype(BF16)
        for g in range(N_SSM_GROUPS):
            rt_ref[0, g * CHUNK_W + t * SSM_GROUP:g * CHUNK_W + (t + 1) * SSM_GROUP, :] = (
                slab[g * SSM_GROUP:(g + 1) * SSM_GROUP, :])
    ns = SSM_STATE
    for m in range(N_SSM_GROUPS // 2):
        sl = [jnp.dot(in_ref[g], rt_ref[0, g * CHUNK_W:(g + 1) * CHUNK_W, :], preferred_element_type=F32)
              for g in (2 * m, 2 * m + 1)]
        for d in range(2):
            pair = jnp.concatenate([sl[0][2 * d * ns:(2 * d + 1) * ns], sl[1][2 * d * ns:(2 * d + 1) * ns],
                                    sl[0][(2 * d + 1) * ns:(2 * d + 2) * ns],
                                    sl[1][(2 * d + 1) * ns:(2 * d + 2) * ns]], axis=0).T
            sloc_ref[0, 2 * d, :, m * LANES:(m + 1) * LANES] = pair[:, :LANES]
            sloc_ref[0, 2 * d + 1, :, m * LANES:(m + 1) * LANES] = pair[:, LANES:]


def _ssm_local(hb, wu_t, in_t, ncp):
    bsz, tp, _ = hb.shape
    n_chunks = tp // CHUNK
    hv = hb.reshape(bsz, n_chunks, CHUNK * D_MODEL)
    wl = CHUNK_LANES
    return pl.pallas_call(
        functools.partial(_ssm_local_kernel, n_chunks),
        grid=(bsz, ncp // wl),
        in_specs=[pl.BlockSpec((1, wl, CHUNK * D_MODEL), lambda b, c: (b, c, 0)),
                  pl.BlockSpec(wu_t.shape, lambda b, c: (0, 0)),
                  pl.BlockSpec(in_t.shape, lambda b, c: (0, 0, 0))],
        out_specs=[pl.BlockSpec((1, N_SSM_GROUPS * CHUNK_W, wl), lambda b, c: (b, 0, c)),
                   pl.BlockSpec((1, 4, wl, N_STATE_ROWS), lambda b, c: (b, 0, c, 0))],
        out_shape=[jax.ShapeDtypeStruct((bsz, N_SSM_GROUPS * CHUNK_W, ncp), BF16),
                   jax.ShapeDtypeStruct((bsz, 4, ncp, N_STATE_ROWS), F32)],
        compiler_params=_cparams(("parallel", "parallel")),
    )(hv, wu_t, in_t)


def _ssm_scan_kernel(sloc_ref, tab_ref, car_ref, st_sc):
    ncp, lb = sloc_ref.shape[2], sloc_ref.shape[3]
    n_groups = ncp // SUBLANES
    row = lax.broadcasted_iota(jnp.int32, (ncp, 1), 0)
    sub = row & (SUBLANES - 1)

    for c0, reverse in ((0, False), (2, True)):
        re, im = sloc_ref[0, c0], sloc_ref[0, c0 + 1]
        for s in range(3):
            d = 1 << s
            keep = (sub < SUBLANES - d) if reverse else (sub >= d)
            shift = ncp - d if reverse else d
            r_re = jnp.where(keep, pltpu.roll(re, shift, 0), 0.0)
            r_im = jnp.where(keep, pltpu.roll(im, shift, 0), 0.0)
            a_re = tab_ref[c0, SUBLANES + s:SUBLANES + s + 1, :]
            a_im = tab_ref[c0 + 1, SUBLANES + s:SUBLANES + s + 1, :]
            re, im = re + (a_re * r_re - a_im * r_im), im + (a_re * r_im + a_im * r_re)
        st_sc[0] = re
        st_sc[1] = im
        p_re, p_im = tab_ref[c0, :SUBLANES, :], tab_ref[c0 + 1, :SUBLANES, :]
        edge = 0 if reverse else SUBLANES - 1

        def body(i, carry):
            c_re, c_im = carry
            grp = n_groups - 1 - i if reverse else i
            off = pl.multiple_of(grp * SUBLANES, SUBLANES)
            x_re = st_sc[0, pl.ds(off, SUBLANES), :] + (p_re * c_re - p_im * c_im)
            x_im = st_sc[1, pl.ds(off, SUBLANES), :] + (p_re * c_im + p_im * c_re)
            st_sc[0, pl.ds(off, SUBLANES), :] = x_re
            st_sc[1, pl.ds(off, SUBLANES), :] = x_im
            return x_re[edge:edge + 1], x_im[edge:edge + 1]

        zero = jnp.zeros((1, lb), F32)
        lax.fori_loop(0, n_groups, body, (zero, zero))
        keep = (row < ncp - 1) if reverse else (row >= 1)
        shift = ncp - 1 if reverse else 1
        car_ref[0, c0] = jnp.where(keep, pltpu.roll(st_sc[0], shift, 0), 0.0).astype(BF16)
        car_ref[0, c0 + 1] = jnp.where(keep, pltpu.roll(st_sc[1], shift, 0), 0.0).astype(BF16)


def _ssm_scan(sloc, tab, lb):
    bsz, _, ncp, _ = sloc.shape
    return pl.pallas_call(
        _ssm_scan_kernel,
        grid=(bsz, N_STATE_ROWS // lb),
        in_specs=[pl.BlockSpec((1, 4, ncp, lb), lambda b, r: (b, 0, 0, r)),
                  pl.BlockSpec((4, 2 * SUBLANES, lb), lambda b, r: (0, 0, r))],
        out_specs=pl.BlockSpec((1, 4, ncp, lb), lambda b, r: (b, 0, 0, r)),
        out_shape=jax.ShapeDtypeStruct((bsz, 4, ncp, N_STATE_ROWS), BF16),
        scratch_shapes=[pltpu.VMEM((2, ncp, lb), F32)],
        compiler_params=_cparams(("parallel", "parallel")),
    )(sloc, tab)


def _ssm_out_kernel(rt_ref, car_ref, m_ref, o_ref, wg_ref, bg_ref, out_ref, y_sc):
    wl = rt_ref.shape[2]
    nt = (((1,), (1,)), ((), ()))
    for m in range(N_SSM_GROUPS // 2):
        sc = jnp.concatenate([car_ref[0, c, :, m * LANES:(m + 1) * LANES] for c in range(4)], axis=1)
        y_car = lax.dot_general(o_ref[m], sc, nt, preferred_element_type=F32)
        for i in range(2):
            g = 2 * m + i
            rg = rt_ref[0, g * CHUNK_W:(g + 1) * CHUNK_W, :]
            y = jnp.dot(m_ref[g], rg, preferred_element_type=F32) + y_car[i * CHUNK_W:(i + 1) * CHUNK_W]
            y = jax.nn.gelu(y).astype(y_sc.dtype)
            for t in range(CHUNK):
                y_sc[t * SSM_WIDTH + g * SSM_GROUP:t * SSM_WIDTH + (g + 1) * SSM_GROUP, :] = (
                    y[t * SSM_GROUP:(t + 1) * SSM_GROUP, :])
    for t in range(CHUNK):
        slab = y_sc[t * SSM_WIDTH:(t + 1) * SSM_WIDTH, :]
        z = jnp.dot(wg_ref[...], slab, preferred_element_type=F32) + bg_ref[...]
        o = slab.astype(F32) * jax.nn.sigmoid(z)
        o_t = o.T
        for cb in range(SSM_WIDTH // LANES):
            out_ref[0, cb, pl.ds(t, wl, stride=CHUNK), :] = o_t[:, cb * LANES:(cb + 1) * LANES]


def _ssm_out(rt, car, m_t, o_t, wglu_t, bglu, tp):
    bsz, _, ncp = rt.shape
    wl = CHUNK_LANES
    once = pl.Buffered(1)
    return pl.pallas_call(
        _ssm_out_kernel,
        grid=(bsz, ncp // wl),
        in_specs=[pl.BlockSpec((1, N_SSM_GROUPS * CHUNK_W, wl), lambda b, c: (b, 0, c)),
                  pl.BlockSpec((1, 4, wl, N_STATE_ROWS), lambda b, c: (b, 0, c, 0)),
                  pl.BlockSpec(m_t.shape, lambda b, c: (0, 0, 0), pipeline_mode=once),
                  pl.BlockSpec(o_t.shape, lambda b, c: (0, 0, 0), pipeline_mode=once),
                  pl.BlockSpec(wglu_t.shape, lambda b, c: (0, 0)),
                  pl.BlockSpec(bglu.shape, lambda b, c: (0, 0))],
        out_specs=pl.BlockSpec((1, SSM_WIDTH // LANES, wl * CHUNK, LANES), lambda b, c: (b, 0, c, 0)),
        out_shape=jax.ShapeDtypeStruct((bsz, SSM_WIDTH // LANES, tp, LANES), F32),
        scratch_shapes=[pltpu.VMEM((CHUNK * SSM_WIDTH, wl), BF16)],
        compiler_params=_cparams(("parallel", "parallel"), SSM_OUT_VMEM_LIMIT),
    )(rt, car, m_t, o_t, wglu_t, bglu)


def _ssm_operators(lam_re, lam_im, log_dt, b_re, b_im, c_re, c_im, d_skip):
    g_n, p_n, h_n = N_SSM_GROUPS, SSM_STATE, SSM_GROUP
    hi = lax.Precision.HIGHEST
    kern, w_in, w_out, tab = [], [], [], []
    fwd_pow = jnp.arange(1, SUBLANES + 1, dtype=F32)
    level1_pow = jnp.array([1.0, 2.0, 4.0] + [0.0] * (SUBLANES - 3), F32)
    for direction in range(2):
        l_re, l_im = lam_re[direction].astype(F32), lam_im[direction].astype(F32)
        dt = jnp.exp(log_dt[direction].astype(F32))[:, None]
        ldt_re, ldt_im = l_re * dt, l_im * dt
        j = jnp.arange(CHUNK + 1, dtype=F32)[:, None, None]
        mag = jnp.exp(j * ldt_re[None])
        p_re, p_im = mag * jnp.cos(j * ldt_im[None]), mag * jnp.sin(j * ldt_im[None])
        a, b = p_re[1] - 1.0, p_im[1]
        den = l_re * l_re + l_im * l_im
        z_re, z_im = (a * l_re + b * l_im) / den, (b * l_re - a * l_im) / den
        br, bi = b_re[direction].astype(F32), b_im[direction].astype(F32)
        bb_re = z_re[:, :, None] * br - z_im[:, :, None] * bi
        bb_im = z_re[:, :, None] * bi + z_im[:, :, None] * br
        w_re = p_re[:, :, :, None] * bb_re[None] - p_im[:, :, :, None] * bb_im[None]
        w_im = p_re[:, :, :, None] * bb_im[None] + p_im[:, :, :, None] * bb_re[None]
        cr, ci = c_re[direction].astype(F32), c_im[direction].astype(F32)
        kern.append(jnp.einsum('ghp,jgpk->jghk', cr, w_re[:CHUNK], precision=hi)
                    - jnp.einsum('ghp,jgpk->jghk', ci, w_im[:CHUNK], precision=hi))
        w_in.append((w_re, w_im))
        w_out.append((cr[None] * p_re[:, :, None, :] - ci[None] * p_im[:, :, None, :],
                      cr[None] * p_im[:, :, None, :] + ci[None] * p_re[:, :, None, :]))
        scan_pow = CHUNK * jnp.concatenate([fwd_pow if direction == 0 else fwd_pow[::-1], level1_pow])
        d_mag = jnp.exp(scan_pow[:, None] * ldt_re.reshape(1, -1))
        d_ang = scan_pow[:, None] * ldt_im.reshape(1, -1)
        tab += [d_mag * jnp.cos(d_ang), d_mag * jnp.sin(d_ang)]
    t_idx = jnp.arange(CHUNK)
    diff = t_idx[:, None] - t_idx[None, :]
    kf = jnp.where((diff >= 0)[:, :, None, None, None], kern[0][jnp.clip(diff, 0, CHUNK - 1)], 0.0)
    kb = jnp.where((diff <= 0)[:, :, None, None, None], kern[1][jnp.clip(-diff, 0, CHUNK - 1)], 0.0)
    skip = (jnp.eye(CHUNK, dtype=F32)[:, :, None, None, None]
            * (jnp.eye(h_n, dtype=F32)[None] * d_skip.astype(F32).reshape(g_n, 1, h_n))[None, None])
    m_t = (kf + kb + skip).transpose(2, 0, 3, 1, 4).reshape(g_n, CHUNK_W, CHUNK_W)
    in_parts = []
    for direction in range(2):
        for part in w_in[direction]:
            sel = part[:CHUNK][::-1] if direction == 0 else part[:CHUNK]
            in_parts.append(sel.transpose(1, 2, 0, 3).reshape(g_n, p_n, CHUNK_W))
    in_t = jnp.concatenate(in_parts, axis=1)
    out_parts = []
    for direction in range(2):
        for sign, part in zip((1.0, -1.0), w_out[direction]):
            sel = part[1:] if direction == 0 else part[1:][::-1]
            out_parts.append(sign * sel.transpose(1, 0, 2, 3).reshape(g_n, CHUNK_W, p_n))
    eye2 = jnp.eye(2, dtype=F32)
    o_pair = jnp.stack([part.reshape(g_n // 2, 2, CHUNK_W, p_n) for part in out_parts], axis=3)
    o_pair = (o_pair[:, :, :, :, None, :] * eye2[None, :, None, None, :, None]).reshape(
        g_n // 2, 2 * CHUNK_W, 4 * 2 * p_n)
    return m_t.astype(BF16), in_t.astype(BF16), o_pair.astype(BF16), jnp.stack(tab)


def _outproj_kernel(attn_ref, ssm_ref, h_ref, wo_ref, ga_ref, gs_ref, lg_ref, lb_ref,
                    wr_ref, br_ref, x1_ref, gate_ref):
    a = attn_ref[0].astype(F32)
    s = jnp.concatenate([ssm_ref[0, cb] for cb in range(SSM_WIDTH // LANES)], axis=1)
    an = a * lax.rsqrt(jnp.mean(a * a, axis=-1, keepdims=True) + NORM_EPS) * ga_ref[...]
    sn = s * lax.rsqrt(jnp.mean(s * s, axis=-1, keepdims=True) + NORM_EPS) * gs_ref[...]
    mixed = (jnp.dot(an.astype(BF16), wo_ref[:ATTN_WIDTH, :], preferred_element_type=F32)
             + jnp.dot(sn.astype(BF16), wo_ref[ATTN_WIDTH:, :], preferred_element_type=F32))
    x1 = _layer_norm(DEEPNORM_ALPHA * h_ref[0] + mixed, lg_ref[...], lb_ref[...])
    x1_ref[0] = x1

    logits = jnp.dot(x1.astype(BF16), wr_ref[...], preferred_element_type=F32) + br_ref[...]
    lane = lax.broadcasted_iota(jnp.int32, logits.shape, 1)
    neg = -jnp.inf
    is_group = (lane >= N_EXPERTS) & (lane < N_EXPERTS + N_EXPERT_GROUPS)
    glog = jnp.where(is_group, logits, neg)
    gmax = jnp.max(glog, axis=1, keepdims=True)
    g_val = 1.0 / jnp.sum(jnp.exp(glog - gmax), axis=1, keepdims=True)
    g_idx = jnp.min(jnp.where(glog == gmax, lane, LANES), axis=1, keepdims=True) - N_EXPERTS
    in_group = (lane < N_EXPERTS) & ((lane >> 2) == g_idx)
    el = jnp.where(in_group, logits, neg)
    v1 = jnp.max(el, axis=1, keepdims=True)
    i1 = jnp.min(jnp.where(el == v1, lane, LANES), axis=1, keepdims=True)
    el2 = jnp.where(lane == i1, neg, el)
    v2 = jnp.max(el2, axis=1, keepdims=True)
    i2 = jnp.min(jnp.where(el2 == v2, lane, LANES), axis=1, keepdims=True)
    e2 = jnp.exp(v2 - v1)
    w1 = g_val / (1.0 + e2)
    gate_ref[0] = jnp.where(lane == i1, w1, jnp.where(lane == i2, w1 * e2, 0.0))


def _outproj(attn, ssm, h, wo, ga, gs, lg, lb, wr, br, tm):
    bsz, tp, _ = h.shape
    row = lambda b, j: (b, j, 0)
    c2 = lambda b, j: (0, 0)
    return pl.pallas_call(
        _outproj_kernel,
        grid=(bsz, tp // tm),
        in_specs=[pl.BlockSpec((1, tm, ATTN_WIDTH), row),
                  pl.BlockSpec((1, SSM_WIDTH // LANES, tm, LANES), lambda b, j: (b, 0, j, 0)),
                  pl.BlockSpec((1, tm, D_MODEL), row), pl.BlockSpec(wo.shape, c2),
                  pl.BlockSpec(ga.shape, c2), pl.BlockSpec(gs.shape, c2),
                  pl.BlockSpec(lg.shape, c2), pl.BlockSpec(lb.shape, c2),
                  pl.BlockSpec(wr.shape, c2), pl.BlockSpec(br.shape, c2)],
        out_specs=[pl.BlockSpec((1, tm, D_MODEL), row), pl.BlockSpec((1, tm, LANES), row)],
        out_shape=[jax.ShapeDtypeStruct((bsz, tp, D_MODEL), F32),
                   jax.ShapeDtypeStruct((bsz, tp, LANES), F32)],
        compiler_params=_cparams(("parallel", "parallel")),
    )(attn, ssm, h, wo, ga, gs, lg, lb, wr, br)


def _moe_kernel(x_ref, gate_ref, wgu_ref, wd_ref, lg_ref, lb_ref, y_ref, xb_sc, acc_sc):
    e = pl.program_id(2)

    @pl.when(e == 0)
    def _():
        xb_sc[...] = x_ref[0].astype(BF16)
        acc_sc[...] = jnp.zeros_like(acc_sc)

    hgu = jnp.dot(xb_sc[...], wgu_ref[0], preferred_element_type=F32)
    gates = gate_ref[0]
    lane = lax.broadcasted_iota(jnp.int32, gates.shape, 1)
    gate_e = jnp.sum(jnp.where(lane == e, gates, 0.0), axis=1, keepdims=True)
    hh = jax.nn.silu(hgu[:, :EXPERT_FF]) * hgu[:, EXPERT_FF:] * gate_e
    acc_sc[...] += jnp.dot(hh.astype(BF16), wd_ref[0], preferred_element_type=F32)

    @pl.when(e == N_EXPERTS - 1)
    def _():
        y_ref[0] = _layer_norm(DEEPNORM_ALPHA * x_ref[0] + acc_sc[...], lg_ref[...], lb_ref[...])


def _moe(x1, gates, wgu, wd, lg, lb, tm):
    bsz, tp, _ = x1.shape
    row = lambda b, j, e: (b, j, 0)
    c2 = lambda b, j, e: (0, 0)
    return pl.pallas_call(
        _moe_kernel,
        grid=(bsz, tp // tm, N_EXPERTS),
        in_specs=[pl.BlockSpec((1, tm, D_MODEL), row), pl.BlockSpec((1, tm, LANES), row),
                  pl.BlockSpec((1, D_MODEL, 2 * EXPERT_FF), lambda b, j, e: (e, 0, 0)),
                  pl.BlockSpec((1, EXPERT_FF, D_MODEL), lambda b, j, e: (e, 0, 0)),
                  pl.BlockSpec(lg.shape, c2), pl.BlockSpec(lb.shape, c2)],
        out_specs=pl.BlockSpec((1, tm, D_MODEL), row),
        out_shape=jax.ShapeDtypeStruct((bsz, tp, D_MODEL), F32),
        scratch_shapes=[pltpu.VMEM((tm, D_MODEL), BF16), pltpu.VMEM((tm, D_MODEL), F32)],
        compiler_params=_cparams(("parallel", "parallel", "arbitrary")),
    )(x1, gates, wgu, wd, lg, lb)


def _rope_tables(n_real):
    r = jnp.arange(n_real, dtype=jnp.int32)
    row = jnp.concatenate([jnp.zeros((PREFIX,), F32), (r // GRID_W).astype(F32)])
    col = jnp.concatenate([jnp.zeros((PREFIX,), F32), (r % GRID_W).astype(F32)])
    n_freq = HEAD_DIM // 4
    inv_freq = ROPE_THETA ** (-jnp.arange(n_freq, dtype=F32) / n_freq)
    ang = jnp.concatenate([row[:, None] * inv_freq, col[:, None] * inv_freq], axis=-1)
    cos = jnp.repeat(jnp.cos(ang), 2, axis=-1)
    sin = jnp.repeat(jnp.sin(ang), 2, axis=-1) * jnp.tile(jnp.array([-1.0, 1.0], F32), HEAD_DIM // 2)
    return jnp.tile(cos, (1, 2)), jnp.tile(sin, (1, 2))


def _row_tile(tp, cap):
    best = 8
    for t in range(8, cap + 1, 8):
        if tp % t == 0:
            best = t
    return best


def _layer_weights(p, l):
    w_in = p['w_in'][l]
    kv_end = ATTN_WIDTH + 2 * KV_WIDTH
    m_t, in_t, o_t, tab = _ssm_operators(p['ssm_lambda_re'][l], p['ssm_lambda_im'][l], p['ssm_log_dt'][l],
                                         p['ssm_b_re'][l], p['ssm_b_im'][l], p['ssm_c_re'][l],
                                         p['ssm_c_im'][l], p['ssm_d'][l])
    head_avg = jnp.kron(jnp.eye(N_Q_HEADS, dtype=F32), jnp.full((HEAD_DIM, HEAD_DIM), 1.0 / HEAD_DIM, F32))
    wr = jnp.zeros((D_MODEL, LANES), F32)
    wr = wr.at[:, :N_EXPERTS].set(p['w_router'][l]).at[:, N_EXPERTS:N_EXPERTS + N_EXPERT_GROUPS].set(p['w_group'][l])
    br = jnp.zeros((1, LANES), F32)
    br = br.at[0, :N_EXPERTS].set(p['b_router'][l]).at[0, N_EXPERTS:N_EXPERTS + N_EXPERT_GROUPS].set(p['b_group'][l])
    return dict(
        w_qkv=w_in[:, :kv_end].astype(BF16),
        gq=(jnp.tile(p['q_norm_g'][l], N_Q_HEADS) * (HEAD_DIM ** -0.5)).reshape(1, ATTN_WIDTH),
        gk=jnp.tile(p['k_norm_g'][l], N_KV_HEADS).reshape(1, KV_WIDTH),
        bdq=head_avg.astype(BF16), bdk=head_avg[:KV_WIDTH, :KV_WIDTH].astype(BF16),
        wu_t=w_in[:, kv_end:].T.astype(BF16),
        m_t=m_t, in_t=in_t, o_t=o_t, tab=tab,
        wglu_t=p['w_glu'][l].T.astype(BF16), bglu=p['b_glu'][l].reshape(SSM_WIDTH, 1).astype(F32),
        wo=p['w_out'][l].astype(BF16),
        ga=p['attn_out_g'][l].reshape(1, -1), gs=p['ssm_out_g'][l].reshape(1, -1),
        ln1g=p['ln1_g'][l].reshape(1, -1), ln1b=p['ln1_b'][l].reshape(1, -1),
        wr=wr.astype(BF16), br=br,
        wgu=jnp.concatenate([p['w_gate'][l], p['w_up'][l]], axis=-1).astype(BF16),
        wd=p['w_down'][l].astype(BF16),
        ln2g=p['ln2_g'][l].reshape(1, -1), ln2b=p['ln2_b'][l].reshape(1, -1),
    )


def _trunk(x, p, weights):
    bsz, n_real, _ = x.shape
    tp = n_real + PREFIX
    meta = jnp.broadcast_to(p['meta_tokens'][None].astype(x.dtype), (bsz, N_META, D_MODEL))
    h = jnp.concatenate([jnp.zeros((bsz, N_PAD, D_MODEL), x.dtype), meta, x], axis=1)
    cos_t, sin_t = _rope_tables(n_real)
    n_chunks = tp // CHUNK
    ncp = -(-n_chunks // CHUNK_LANES) * CHUNK_LANES
    tm = _row_tile(tp, 1408)
    ln = (p['ln_in_g'].reshape(1, -1), p['ln_in_b'].reshape(1, -1))
    for l in range(DEPTH):
        w = weights[l]
        h, hb, q, k, v = _inproj(h, cos_t, sin_t, ln if l == 0 else None, w['w_qkv'], w['gq'], w['gk'],
                                 w['bdq'], w['bdk'], tm)
        attn = _attention(q, k, v)
        rt, sloc = _ssm_local(hb, w['wu_t'], w['in_t'], ncp)
        car = _ssm_scan(sloc, w['tab'], SCAN_LANES)
        ssm = _ssm_out(rt, car, w['m_t'], w['o_t'], w['wglu_t'], w['bglu'], tp)
        x1, gates = _outproj(attn, ssm, h, w['wo'], w['ga'], w['gs'], w['ln1g'], w['ln1b'],
                             w['wr'], w['br'], tm)
        h = _moe(x1, gates, w['wgu'], w['wd'], w['ln2g'], w['ln2b'], tm)
    return h[:, PREFIX:]


def kernel(x_prompt, x_sample, meta_tokens, ln_in_g, ln_in_b, w_in, q_norm_g, k_norm_g, ssm_lambda_re, ssm_lambda_im, ssm_log_dt, ssm_b_re, ssm_b_im, ssm_c_re, ssm_c_im, ssm_d, w_glu, b_glu, attn_out_g, ssm_out_g, w_out, ln1_g, ln1_b, w_group, b_group, w_router, b_router, w_gate, w_up, w_down, ln2_g, ln2_b):
    p = dict(meta_tokens=meta_tokens, ln_in_g=ln_in_g, ln_in_b=ln_in_b, w_in=w_in,
             q_norm_g=q_norm_g, k_norm_g=k_norm_g, ssm_lambda_re=ssm_lambda_re,
             ssm_lambda_im=ssm_lambda_im, ssm_log_dt=ssm_log_dt, ssm_b_re=ssm_b_re,
             ssm_b_im=ssm_b_im, ssm_c_re=ssm_c_re, ssm_c_im=ssm_c_im, ssm_d=ssm_d,
             w_glu=w_glu, b_glu=b_glu, attn_out_g=attn_out_g, ssm_out_g=ssm_out_g,
             w_out=w_out, ln1_g=ln1_g, ln1_b=ln1_b, w_group=w_group, b_group=b_group,
             w_router=w_router, b_router=b_router, w_gate=w_gate, w_up=w_up,
             w_down=w_down, ln2_g=ln2_g, ln2_b=ln2_b)
    weights = [_layer_weights(p, l) for l in range(DEPTH)]
    return (_trunk(x_prompt, p, weights), _trunk(x_sample, p, weights))
```

```python
import functools

import jax
import jax.numpy as jnp
from jax import lax
from jax.experimental import pallas as pl
from jax.experimental.pallas import tpu as pltpu

D_MODEL = 1024
N_META = 16
GRID_W = 64
ATTN_WIDTH = 512
SSM_WIDTH = 512
HEAD_DIM = 64
N_Q_HEADS = 8
N_KV_HEADS = 2
Q_PER_KV = 4
KV_WIDTH = 128
ROPE_THETA = 10000.0
SSM_GROUP = 16
N_SSM_GROUPS = 32
SSM_STATE = 64
N_EXPERT_GROUPS = 4
EXPERTS_PER_GROUP = 4
N_EXPERTS = 16
EXPERT_FF = 256
DEPTH = 2
DEEPNORM_ALPHA = (2.0 * DEPTH) ** 0.25
NORM_EPS = 1e-6

LANES = 128
SUBLANES = 8
PREFIX = 128
N_PAD = PREFIX - N_META
CHUNK = 16
CHUNK_W = CHUNK * SSM_GROUP
STATE_W = 4 * SSM_STATE
N_STATE_ROWS = N_SSM_GROUPS * SSM_STATE
CHUNK_LANES = 256
PAD_CHUNKS = N_PAD // CHUNK
SCAN_LANES = 512
Q_TILE = 128
K_TILE = 2048
VMEM_LIMIT = 56 * 1024 * 1024
SSM_OUT_VMEM_LIMIT = 60 * 1024 * 1024

F32 = jnp.float32
BF16 = jnp.bfloat16


def _cparams(sem, vmem_limit=VMEM_LIMIT):
    return pltpu.CompilerParams(dimension_semantics=sem, vmem_limit_bytes=vmem_limit)


def _layer_norm(x, g, b):
    mu = jnp.mean(x, axis=-1, keepdims=True)
    xc = x - mu
    var = jnp.mean(xc * xc, axis=-1, keepdims=True)
    return xc * lax.rsqrt(var + NORM_EPS) * g + b


def _swap_pairs(x):
    n = x.shape[-1]
    lane = lax.broadcasted_iota(jnp.int32, x.shape, x.ndim - 1)
    nxt = pltpu.roll(x, n - 1, x.ndim - 1)
    prv = pltpu.roll(x, 1, x.ndim - 1)
    return jnp.where((lane & 1) == 0, nxt, prv)


def _inproj_kernel(pre_ln, tm, *refs):
    if pre_ln:
        (x_ref, cos_ref, sin_ref, lng_ref, lnb_ref, w_ref, gq_ref, gk_ref, bdq_ref, bdk_ref,
         h_ref, u_ref, q_ref, k_ref, v_ref) = refs
    else:
        (x_ref, cos_ref, sin_ref, w_ref, gq_ref, gk_ref, bdq_ref, bdk_ref,
         u_ref, q_ref, k_ref, v_ref) = refs
    x = x_ref[0]
    if pre_ln:
        h = _layer_norm(x, lng_ref[...], lnb_ref[...])
        h_ref[0] = h
    else:
        h = x
    proj = jnp.dot(h.astype(BF16), w_ref[...], preferred_element_type=F32)
    q = proj[:, :ATTN_WIDTH]
    k = proj[:, ATTN_WIDTH:ATTN_WIDTH + KV_WIDTH]
    v = proj[:, ATTN_WIDTH + KV_WIDTH:ATTN_WIDTH + 2 * KV_WIDTH]
    for cb in range(SSM_WIDTH // LANES):
        base = ATTN_WIDTH + 2 * KV_WIDTH + cb * LANES
        u_ref[0, cb] = proj[:, base:base + LANES]
    cos = cos_ref[...]
    sin = sin_ref[...]

    q_ms = jnp.dot((q * q).astype(BF16), bdq_ref[...], preferred_element_type=F32)
    qn = q * lax.rsqrt(q_ms + NORM_EPS) * gq_ref[...]
    cos4 = jnp.concatenate([cos] * 4, axis=1)
    sin4 = jnp.concatenate([sin] * 4, axis=1)
    qr = qn * cos4 + _swap_pairs(qn) * sin4
    qr_odd = pltpu.roll(qr, ATTN_WIDTH - HEAD_DIM, 1)
    for hh in range(N_Q_HEADS):
        src = qr if hh % 2 == 0 else qr_odd
        base = (hh // 2) * LANES
        q_ref[0, hh] = src[:, base:base + HEAD_DIM].astype(BF16)

    k_ms = jnp.dot((k * k).astype(BF16), bdk_ref[...], preferred_element_type=F32)
    kn = k * lax.rsqrt(k_ms + NORM_EPS) * gk_ref[...]
    kr = kn * cos + _swap_pairs(kn) * sin
    k_ref[0, 0] = kr[:, :HEAD_DIM].astype(BF16)
    k_ref[0, 1] = pltpu.roll(kr, HEAD_DIM, 1)[:, :HEAD_DIM].astype(BF16)

    pos = pl.program_id(1) * tm + lax.broadcasted_iota(jnp.int32, (tm, 1), 0)
    valid = (pos >= N_PAD).astype(F32)
    lane = lax.broadcasted_iota(jnp.int32, (tm, KV_WIDTH), 1)
    ones_col = (lane == HEAD_DIM).astype(F32)
    v_sh = pltpu.roll(v, HEAD_DIM, 1)
    v_ref[0, 0] = (jnp.where(lane < HEAD_DIM, v, ones_col) * valid).astype(BF16)
    v_ref[0, 1] = (jnp.where(lane < HEAD_DIM, v_sh, ones_col) * valid).astype(BF16)


def _inproj(x, cos_t, sin_t, ln, w_in, gq, gk, bdq, bdk, tm):
    bsz, tp, _ = x.shape
    pre_ln = ln is not None
    grid = (bsz, tp // tm)
    row = lambda b, j: (b, j, 0)
    const2 = lambda b, j: (0, 0)
    in_specs = [pl.BlockSpec((1, tm, D_MODEL), row),
                pl.BlockSpec((tm, LANES), lambda b, j: (j, 0)),
                pl.BlockSpec((tm, LANES), lambda b, j: (j, 0))]
    args = [x, cos_t, sin_t]
    if pre_ln:
        in_specs += [pl.BlockSpec((1, D_MODEL), const2)] * 2
        args += [ln[0], ln[1]]
    in_specs += [pl.BlockSpec(w_in.shape, const2), pl.BlockSpec(gq.shape, const2),
                 pl.BlockSpec(gk.shape, const2), pl.BlockSpec(bdq.shape, const2),
                 pl.BlockSpec(bdk.shape, const2)]
    args += [w_in, gq, gk, bdq, bdk]
    head4 = lambda b, j: (b, 0, j, 0)
    out_shape = [jax.ShapeDtypeStruct((bsz, SSM_WIDTH // LANES, tp, LANES), F32),
                 jax.ShapeDtypeStruct((bsz, N_Q_HEADS, tp, HEAD_DIM), BF16),
                 jax.ShapeDtypeStruct((bsz, N_KV_HEADS, tp, HEAD_DIM), BF16),
                 jax.ShapeDtypeStruct((bsz, N_KV_HEADS, tp, LANES), BF16)]
    out_specs = [pl.BlockSpec((1, SSM_WIDTH // LANES, tm, LANES), head4),
                 pl.BlockSpec((1, N_Q_HEADS, tm, HEAD_DIM), head4),
                 pl.BlockSpec((1, N_KV_HEADS, tm, HEAD_DIM), head4),
                 pl.BlockSpec((1, N_KV_HEADS, tm, LANES), head4)]
    if pre_ln:
        out_shape = [jax.ShapeDtypeStruct((bsz, tp, D_MODEL), F32)] + out_shape
        out_specs = [pl.BlockSpec((1, tm, D_MODEL), row)] + out_specs
    outs = pl.pallas_call(
        functools.partial(_inproj_kernel, pre_ln, tm),
        grid=grid, in_specs=in_specs, out_specs=out_specs, out_shape=out_shape,
        compiler_params=_cparams(("parallel", "parallel")),
    )(*args)
    if pre_ln:
        return outs
    return [x] + list(outs)


def _attn_kernel(n_steps, q_ref, k_ref, v_ref, o_ref, m_sc, acc_sc):
    tq = q_ref.shape[2]
    rows = Q_PER_KV * tq
    nt = (((1,), (1,)), ((), ()))
    q = [q_ref[0, j * Q_PER_KV:(j + 1) * Q_PER_KV].reshape(rows, HEAD_DIM) for j in range(N_KV_HEADS)]

    def step(j, kt, vt, first):
        s = lax.dot_general(q[j], kt, nt, preferred_element_type=F32)
        m_cur = jnp.broadcast_to(jnp.max(s, axis=1, keepdims=True), (rows, LANES))
        m_new = m_cur if first else jnp.maximum(m_sc[j], m_cur)
        p = jnp.exp(s - jnp.tile(m_new, (1, s.shape[1] // LANES)))
        pv = jnp.dot(p.astype(BF16), vt, preferred_element_type=F32)
        if first:
            acc_sc[j] = pv
        else:
            acc_sc[j] = jnp.exp(m_sc[j] - m_new) * acc_sc[j] + pv
        m_sc[j] = m_new

    for j in range(N_KV_HEADS):
        step(j, k_ref[0, j, :PREFIX, :], v_ref[0, j, :PREFIX, :], True)

    def body(i, carry):
        off = pl.multiple_of(PREFIX + i * K_TILE, LANES)
        for j in range(N_KV_HEADS):
            step(j, k_ref[0, j, pl.ds(off, K_TILE), :], v_ref[0, j, pl.ds(off, K_TILE), :], False)
        return carry

    lax.fori_loop(0, n_steps, body, 0, unroll=4 if n_steps % 4 == 0 else 1)
    outs = []
    for j in range(N_KV_HEADS):
        acc = acc_sc[j]
        out = acc[:, :HEAD_DIM] / acc[:, HEAD_DIM:HEAD_DIM + 1]
        outs += [out[r * tq:(r + 1) * tq] for r in range(Q_PER_KV)]
    o_ref[0] = jnp.concatenate(outs, axis=1).astype(BF16)


def _attention(q, k, v):
    bsz, _, tp, _ = q.shape
    n_steps = (tp - PREFIX) // K_TILE
    rows = Q_PER_KV * Q_TILE
    once = pl.Buffered(1)
    return pl.pallas_call(
        functools.partial(_attn_kernel, n_steps),
        grid=(bsz, tp // Q_TILE),
        in_specs=[pl.BlockSpec((1, N_Q_HEADS, Q_TILE, HEAD_DIM), lambda b, i: (b, 0, i, 0)),
                  pl.BlockSpec((1, N_KV_HEADS, tp, HEAD_DIM), lambda b, i: (b, 0, 0, 0), pipeline_mode=once),
                  pl.BlockSpec((1, N_KV_HEADS, tp, LANES), lambda b, i: (b, 0, 0, 0), pipeline_mode=once)],
        out_specs=pl.BlockSpec((1, Q_TILE, ATTN_WIDTH), lambda b, i: (b, i, 0)),
        out_shape=jax.ShapeDtypeStruct((bsz, tp, ATTN_WIDTH), BF16),
        scratch_shapes=[pltpu.VMEM((N_KV_HEADS, rows, LANES), F32),
                        pltpu.VMEM((N_KV_HEADS, rows, LANES), F32)],
        compiler_params=_cparams(("parallel", "arbitrary")),
    )(q, k, v)


def _ssm_local_kernel(n_chunks, u_ref, in_ref, rt_ref, sloc_ref):
    wl = rt_ref.shape[2]
    col = pl.program_id(1) * wl + lax.broadcasted_iota(jnp.int32, (1, wl), 1)
    valid = (col >= PAD_CHUNKS) & (col < n_chunks)
    groups_per_block = LANES // SSM_GROUP
    for t in range(CHUNK):
        for cb in range(SSM_WIDTH // LANES):
            piece = u_ref[0, cb, pl.ds(t, wl, stride=CHUNK), :]
            piece = jnp.where(valid, piece.T, 0.0).astype(BF16)
            for gi in range(groups_per_block):
                g = cb * groups_per_block + gi
                rt_ref[0, g * CHUNK_W + t * SSM_GROUP:g * CHUNK_W + (t + 1) * SSM_GROUP, :] = (
                    piece[gi * SSM_GROUP:(gi + 1) * SSM_GROUP, :])
    ns = SSM_STATE
    for m in range(N_SSM_GROUPS // 2):
        sl = [jnp.dot(in_ref[g], rt_ref[0, g * CHUNK_W:(g + 1) * CHUNK_W, :], preferred_element_type=F32)
              for g in (2 * m, 2 * m + 1)]
        for d in range(2):
            pair = jnp.concatenate([sl[0][2 * d * ns:(2 * d + 1) * ns], sl[1][2 * d * ns:(2 * d + 1) * ns],
                                    sl[0][(2 * d + 1) * ns:(2 * d + 2) * ns],
                                    sl[1][(2 * d + 1) * ns:(2 * d + 2) * ns]], axis=0).T
            sloc_ref[0, 2 * d, :, m * LANES:(m + 1) * LANES] = pair[:, :LANES]
            sloc_ref[0, 2 * d + 1, :, m * LANES:(m + 1) * LANES] = pair[:, LANES:]


def _ssm_local(u4, in_t, ncp):
    bsz, _, tp, _ = u4.shape
    n_chunks = tp // CHUNK
    wl = CHUNK_LANES
    return pl.pallas_call(
        functools.partial(_ssm_local_kernel, n_chunks),
        grid=(bsz, ncp // wl),
        in_specs=[pl.BlockSpec((1, SSM_WIDTH // LANES, wl * CHUNK, LANES), lambda b, c: (b, 0, c, 0)),
                  pl.BlockSpec(in_t.shape, lambda b, c: (0, 0, 0))],
        out_specs=[pl.BlockSpec((1, N_SSM_GROUPS * CHUNK_W, wl), lambda b, c: (b, 0, c)),
                   pl.BlockSpec((1, 4, wl, N_STATE_ROWS), lambda b, c: (b, 0, c, 0))],
        out_shape=[jax.ShapeDtypeStruct((bsz, N_SSM_GROUPS * CHUNK_W, ncp), BF16),
                   jax.ShapeDtypeStruct((bsz, 4, ncp, N_STATE_ROWS), F32)],
        compiler_params=_cparams(("parallel", "parallel")),
    )(u4, in_t)


def _ssm_scan_kernel(sloc_ref, tab_ref, car_ref, st_sc):
    ncp, lb = sloc_ref.shape[2], sloc_ref.shape[3]
    n_groups = ncp // SUBLANES
    row = lax.broadcasted_iota(jnp.int32, (ncp, 1), 0)
    sub = row & (SUBLANES - 1)

    for c0, reverse in ((0, False), (2, True)):
        re, im = sloc_ref[0, c0], sloc_ref[0, c0 + 1]
        for s in range(3):
            d = 1 << s
            keep = (sub < SUBLANES - d) if reverse else (sub >= d)
            shift = ncp - d if reverse else d
            r_re = jnp.where(keep, pltpu.roll(re, shift, 0), 0.0)
            r_im = jnp.where(keep, pltpu.roll(im, shift, 0), 0.0)
            a_re = tab_ref[c0, SUBLANES + s:SUBLANES + s + 1, :]
            a_im = tab_ref[c0 + 1, SUBLANES + s:SUBLANES + s + 1, :]
            re, im = re + (a_re * r_re - a_im * r_im), im + (a_re * r_im + a_im * r_re)
        st_sc[0] = re
        st_sc[1] = im
        p_re, p_im = tab_ref[c0, :SUBLANES, :], tab_ref[c0 + 1, :SUBLANES, :]
        edge = 0 if reverse else SUBLANES - 1

        def body(i, carry):
            c_re, c_im = carry
            grp = n_groups - 1 - i if reverse else i
            off = pl.multiple_of(grp * SUBLANES, SUBLANES)
            x_re = st_sc[0, pl.ds(off, SUBLANES), :] + (p_re * c_re - p_im * c_im)
            x_im = st_sc[1, pl.ds(off, SUBLANES), :] + (p_re * c_im + p_im * c_re)
            st_sc[0, pl.ds(off, SUBLANES), :] = x_re
            st_sc[1, pl.ds(off, SUBLANES), :] = x_im
            return x_re[edge:edge + 1], x_im[edge:edge + 1]

        zero = jnp.zeros((1, lb), F32)
        lax.fori_loop(0, n_groups, body, (zero, zero))
        keep = (row < ncp - 1) if reverse else (row >= 1)
        shift = ncp - 1 if reverse else 1
        car_ref[0, c0] = jnp.where(keep, pltpu.roll(st_sc[0], shift, 0), 0.0).astype(BF16)
        car_ref[0, c0 + 1] = jnp.where(keep, pltpu.roll(st_sc[1], shift, 0), 0.0).astype(BF16)


def _ssm_scan(sloc, tab, lb):
    bsz, _, ncp, _ = sloc.shape
    return pl.pallas_call(
        _ssm_scan_kernel,
        grid=(bsz, N_STATE_ROWS // lb),
        in_specs=[pl.BlockSpec((1, 4, ncp, lb), lambda b, r: (b, 0, 0, r)),
                  pl.BlockSpec((4, 2 * SUBLANES, lb), lambda b, r: (0, 0, r))],
        out_specs=pl.BlockSpec((1, 4, ncp, lb), lambda b, r: (b, 0, 0, r)),
        out_shape=jax.ShapeDtypeStruct((bsz, 4, ncp, N_STATE_ROWS), BF16),
        scratch_shapes=[pltpu.VMEM((2, ncp, lb), F32)],
        compiler_params=_cparams(("parallel", "parallel")),
    )(sloc, tab)


def _ssm_out_kernel(rt_ref, car_ref, m_ref, o_ref, wg_ref, bg_ref, out_ref, y_sc):
    wl = rt_ref.shape[2]
    nt = (((1,), (1,)), ((), ()))
    for m in range(N_SSM_GROUPS // 2):
        sc = jnp.concatenate([car_ref[0, c, :, m * LANES:(m + 1) * LANES] for c in range(4)], axis=1)
        y_car = lax.dot_general(o_ref[m], sc, nt, preferred_element_type=F32)
        for i in range(2):
            g = 2 * m + i
            rg = rt_ref[0, g * CHUNK_W:(g + 1) * CHUNK_W, :]
            y = jnp.dot(m_ref[g], rg, preferred_element_type=F32) + y_car[i * CHUNK_W:(i + 1) * CHUNK_W]
            y = jax.nn.gelu(y).astype(y_sc.dtype)
            for t in range(CHUNK):
                y_sc[t * SSM_WIDTH + g * SSM_GROUP:t * SSM_WIDTH + (g + 1) * SSM_GROUP, :] = (
                    y[t * SSM_GROUP:(t + 1) * SSM_GROUP, :])
    for t in range(CHUNK):
        slab = y_sc[t * SSM_WIDTH:(t + 1) * SSM_WIDTH, :]
        z = jnp.dot(wg_ref[...], slab, preferred_element_type=F32) + bg_ref[...]
        o = slab.astype(F32) * jax.nn.sigmoid(z)
        o_t = o.T
        for cb in range(SSM_WIDTH // LANES):
            out_ref[0, cb, pl.ds(t, wl, stride=CHUNK), :] = o_t[:, cb * LANES:(cb + 1) * LANES]


def _ssm_out(rt, car, m_t, o_t, wglu_t, bglu, tp):
    bsz, _, ncp = rt.shape
    wl = CHUNK_LANES
    once = pl.Buffered(1)
    return pl.pallas_call(
        _ssm_out_kernel,
        grid=(bsz, ncp // wl),
        in_specs=[pl.BlockSpec((1, N_SSM_GROUPS * CHUNK_W, wl), lambda b, c: (b, 0, c)),
                  pl.BlockSpec((1, 4, wl, N_STATE_ROWS), lambda b, c: (b, 0, c, 0)),
                  pl.BlockSpec(m_t.shape, lambda b, c: (0, 0, 0), pipeline_mode=once),
                  pl.BlockSpec(o_t.shape, lambda b, c: (0, 0, 0), pipeline_mode=once),
                  pl.BlockSpec(wglu_t.shape, lambda b, c: (0, 0)),
                  pl.BlockSpec(bglu.shape, lambda b, c: (0, 0))],
        out_specs=pl.BlockSpec((1, SSM_WIDTH // LANES, wl * CHUNK, LANES), lambda b, c: (b, 0, c, 0)),
        out_shape=jax.ShapeDtypeStruct((bsz, SSM_WIDTH // LANES, tp, LANES), F32),
        scratch_shapes=[pltpu.VMEM((CHUNK * SSM_WIDTH, wl), BF16)],
        compiler_params=_cparams(("parallel", "parallel"), SSM_OUT_VMEM_LIMIT),
    )(rt, car, m_t, o_t, wglu_t, bglu)


def _ssm_operators(lam_re, lam_im, log_dt, b_re, b_im, c_re, c_im, d_skip):
    g_n, p_n, h_n = N_SSM_GROUPS, SSM_STATE, SSM_GROUP
    hi = lax.Precision.HIGHEST
    kern, w_in, w_out, tab = [], [], [], []
    fwd_pow = jnp.arange(1, SUBLANES + 1, dtype=F32)
    level1_pow = jnp.array([1.0, 2.0, 4.0] + [0.0] * (SUBLANES - 3), F32)
    for direction in range(2):
        l_re, l_im = lam_re[direction].astype(F32), lam_im[direction].astype(F32)
        dt = jnp.exp(log_dt[direction].astype(F32))[:, None]
        ldt_re, ldt_im = l_re * dt, l_im * dt
        j = jnp.arange(CHUNK + 1, dtype=F32)[:, None, None]
        mag = jnp.exp(j * ldt_re[None])
        p_re, p_im = mag * jnp.cos(j * ldt_im[None]), mag * jnp.sin(j * ldt_im[None])
        a, b = p_re[1] - 1.0, p_im[1]
        den = l_re * l_re + l_im * l_im
        z_re, z_im = (a * l_re + b * l_im) / den, (b * l_re - a * l_im) / den
        br, bi = b_re[direction].astype(F32), b_im[direction].astype(F32)
        bb_re = z_re[:, :, None] * br - z_im[:, :, None] * bi
        bb_im = z_re[:, :, None] * bi + z_im[:, :, None] * br
        w_re = p_re[:, :, :, None] * bb_re[None] - p_im[:, :, :, None] * bb_im[None]
        w_im = p_re[:, :, :, None] * bb_im[None] + p_im[:, :, :, None] * bb_re[None]
        cr, ci = c_re[direction].astype(F32), c_im[direction].astype(F32)
        kern.append(jnp.einsum('ghp,jgpk->jghk', cr, w_re[:CHUNK], precision=hi)
                    - jnp.einsum('ghp,jgpk->jghk', ci, w_im[:CHUNK], precision=hi))
        w_in.append((w_re, w_im))
        w_out.append((cr[None] * p_re[:, :, None, :] - ci[None] * p_im[:, :, None, :],
                      cr[None] * p_im[:, :, None, :] + ci[None] * p_re[:, :, None, :]))
        scan_pow = CHUNK * jnp.concatenate([fwd_pow if direction == 0 else fwd_pow[::-1], level1_pow])
        d_mag = jnp.exp(scan_pow[:, None] * ldt_re.reshape(1, -1))
        d_ang = scan_pow[:, None] * ldt_im.reshape(1, -1)
        tab += [d_mag * jnp.cos(d_ang), d_mag * jnp.sin(d_ang)]
    t_idx = jnp.arange(CHUNK)
    diff = t_idx[:, None] - t_idx[None, :]
    kf = jnp.where((diff >= 0)[:, :, None, None, None], kern[0][jnp.clip(diff, 0, CHUNK - 1)], 0.0)
    kb = jnp.where((diff <= 0)[:, :, None, None, None], kern[1][jnp.clip(-diff, 0, CHUNK - 1)], 0.0)
    skip = (jnp.eye(CHUNK, dtype=F32)[:, :, None, None, None]
            * (jnp.eye(h_n, dtype=F32)[None] * d_skip.astype(F32).reshape(g_n, 1, h_n))[None, None])
    m_t = (kf + kb + skip).transpose(2, 0, 3, 1, 4).reshape(g_n, CHUNK_W, CHUNK_W)
    in_parts = []
    for direction in range(2):
        for part in w_in[direction]:
            sel = part[:CHUNK][::-1] if direction == 0 else part[:CHUNK]
            in_parts.append(sel.transpose(1, 2, 0, 3).reshape(g_n, p_n, CHUNK_W))
    in_t = jnp.concatenate(in_parts, axis=1)
    out_parts = []
    for direction in range(2):
        for sign, part in zip((1.0, -1.0), w_out[direction]):
            sel = part[1:] if direction == 0 else part[1:][::-1]
            out_parts.append(sign * sel.transpose(1, 0, 2, 3).reshape(g_n, CHUNK_W, p_n))
    eye2 = jnp.eye(2, dtype=F32)
    o_pair = jnp.stack([part.reshape(g_n // 2, 2, CHUNK_W, p_n) for part in out_parts], axis=3)
    o_pair = (o_pair[:, :, :, :, None, :] * eye2[None, :, None, None, :, None]).reshape(
        g_n // 2, 2 * CHUNK_W, 4 * 2 * p_n)
    return m_t.astype(BF16), in_t.astype(BF16), o_pair.astype(BF16), jnp.stack(tab)


def _outproj_kernel(attn_ref, ssm_ref, h_ref, wo_ref, ga_ref, gs_ref, lg_ref, lb_ref,
                    wr_ref, br_ref, x1_ref, gate_ref):
    a = attn_ref[0].astype(F32)
    s = jnp.concatenate([ssm_ref[0, cb] for cb in range(SSM_WIDTH // LANES)], axis=1)
    an = a * lax.rsqrt(jnp.mean(a * a, axis=-1, keepdims=True) + NORM_EPS) * ga_ref[...]
    sn = s * lax.rsqrt(jnp.mean(s * s, axis=-1, keepdims=True) + NORM_EPS) * gs_ref[...]
    mixed = (jnp.dot(an.astype(BF16), wo_ref[:ATTN_WIDTH, :], preferred_element_type=F32)
             + jnp.dot(sn.astype(BF16), wo_ref[ATTN_WIDTH:, :], preferred_element_type=F32))
    x1 = _layer_norm(DEEPNORM_ALPHA * h_ref[0] + mixed, lg_ref[...], lb_ref[...])
    x1_ref[0] = x1

    logits = jnp.dot(x1.astype(BF16), wr_ref[...], preferred_element_type=F32) + br_ref[...]
    lane = lax.broadcasted_iota(jnp.int32, logits.shape, 1)
    neg = -jnp.inf
    is_group = (lane >= N_EXPERTS) & (lane < N_EXPERTS + N_EXPERT_GROUPS)
    glog = jnp.where(is_group, logits, neg)
    gmax = jnp.max(glog, axis=1, keepdims=True)
    g_val = 1.0 / jnp.sum(jnp.exp(glog - gmax), axis=1, keepdims=True)
    g_idx = jnp.min(jnp.where(glog == gmax, lane, LANES), axis=1, keepdims=True) - N_EXPERTS
    in_group = (lane < N_EXPERTS) & ((lane >> 2) == g_idx)
    el = jnp.where(in_group, logits, neg)
    v1 = jnp.max(el, axis=1, keepdims=True)
    i1 = jnp.min(jnp.where(el == v1, lane, LANES), axis=1, keepdims=True)
    el2 = jnp.where(lane == i1, neg, el)
    v2 = jnp.max(el2, axis=1, keepdims=True)
    i2 = jnp.min(jnp.where(el2 == v2, lane, LANES), axis=1, keepdims=True)
    e2 = jnp.exp(v2 - v1)
    w1 = g_val / (1.0 + e2)
    gate_ref[0] = jnp.where(lane == i1, w1, jnp.where(lane == i2, w1 * e2, 0.0))


def _outproj(attn, ssm, h, wo, ga, gs, lg, lb, wr, br, tm):
    bsz, tp, _ = h.shape
    row = lambda b, j: (b, j, 0)
    c2 = lambda b, j: (0, 0)
    return pl.pallas_call(
        _outproj_kernel,
        grid=(bsz, tp // tm),
        in_specs=[pl.BlockSpec((1, tm, ATTN_WIDTH), row),
                  pl.BlockSpec((1, SSM_WIDTH // LANES, tm, LANES), lambda b, j: (b, 0, j, 0)),
                  pl.BlockSpec((1, tm, D_MODEL), row), pl.BlockSpec(wo.shape, c2),
                  pl.BlockSpec(ga.shape, c2), pl.BlockSpec(gs.shape, c2),
                  pl.BlockSpec(lg.shape, c2), pl.BlockSpec(lb.shape, c2),
                  pl.BlockSpec(wr.shape, c2), pl.BlockSpec(br.shape, c2)],
        out_specs=[pl.BlockSpec((1, tm, D_MODEL), row), pl.BlockSpec((1, tm, LANES), row)],
        out_shape=[jax.ShapeDtypeStruct((bsz, tp, D_MODEL), F32),
                   jax.ShapeDtypeStruct((bsz, tp, LANES), F32)],
        compiler_params=_cparams(("parallel", "parallel")),
    )(attn, ssm, h, wo, ga, gs, lg, lb, wr, br)


def _moe_kernel(x_ref, gate_ref, wgu_ref, wd_ref, lg_ref, lb_ref, y_ref, xb_sc, acc_sc):
    e = pl.program_id(2)

    @pl.when(e == 0)
    def _():
        xb_sc[...] = x_ref[0].astype(BF16)
        acc_sc[...] = jnp.zeros_like(acc_sc)

    hgu = jnp.dot(xb_sc[...], wgu_ref[0], preferred_element_type=F32)
    gates = gate_ref[0]
    lane = lax.broadcasted_iota(jnp.int32, gates.shape, 1)
    gate_e = jnp.sum(jnp.where(lane == e, gates, 0.0), axis=1, keepdims=True)
    hh = jax.nn.silu(hgu[:, :EXPERT_FF]) * hgu[:, EXPERT_FF:] * gate_e
    acc_sc[...] += jnp.dot(hh.astype(BF16), wd_ref[0], preferred_element_type=F32)

    @pl.when(e == N_EXPERTS - 1)
    def _():
        y_ref[0] = _layer_norm(DEEPNORM_ALPHA * x_ref[0] + acc_sc[...], lg_ref[...], lb_ref[...])


def _moe(x1, gates, wgu, wd, lg, lb, tm):
    bsz, tp, _ = x1.shape
    row = lambda b, j, e: (b, j, 0)
    c2 = lambda b, j, e: (0, 0)
    return pl.pallas_call(
        _moe_kernel,
        grid=(bsz, tp // tm, N_EXPERTS),
        in_specs=[pl.BlockSpec((1, tm, D_MODEL), row), pl.BlockSpec((1, tm, LANES), row),
                  pl.BlockSpec((1, D_MODEL, 2 * EXPERT_FF), lambda b, j, e: (e, 0, 0)),
                  pl.BlockSpec((1, EXPERT_FF, D_MODEL), lambda b, j, e: (e, 0, 0)),
                  pl.BlockSpec(lg.shape, c2), pl.BlockSpec(lb.shape, c2)],
        out_specs=pl.BlockSpec((1, tm, D_MODEL), row),
        out_shape=jax.ShapeDtypeStruct((bsz, tp, D_MODEL), F32),
        scratch_shapes=[pltpu.VMEM((tm, D_MODEL), BF16), pltpu.VMEM((tm, D_MODEL), F32)],
        compiler_params=_cparams(("parallel", "parallel", "arbitrary")),
    )(x1, gates, wgu, wd, lg, lb)


def _rope_tables(n_real):
    r = jnp.arange(n_real, dtype=jnp.int32)
    row = jnp.concatenate([jnp.zeros((PREFIX,), F32), (r // GRID_W).astype(F32)])
    col = jnp.concatenate([jnp.zeros((PREFIX,), F32), (r % GRID_W).astype(F32)])
    n_freq = HEAD_DIM // 4
    inv_freq = ROPE_THETA ** (-jnp.arange(n_freq, dtype=F32) / n_freq)
    ang = jnp.concatenate([row[:, None] * inv_freq, col[:, None] * inv_freq], axis=-1)
    cos = jnp.repeat(jnp.cos(ang), 2, axis=-1)
    sin = jnp.repeat(jnp.sin(ang), 2, axis=-1) * jnp.tile(jnp.array([-1.0, 1.0], F32), HEAD_DIM // 2)
    return jnp.tile(cos, (1, 2)), jnp.tile(sin, (1, 2))


def _row_tile(tp, cap):
    best = 8
    for t in range(8, cap + 1, 8):
        if tp % t == 0:
            best = t
    return best


def _layer_weights(p, l):
    w_in = p['w_in'][l]
    m_t, in_t, o_t, tab = _ssm_operators(p['ssm_lambda_re'][l], p['ssm_lambda_im'][l], p['ssm_log_dt'][l],
                                         p['ssm_b_re'][l], p['ssm_b_im'][l], p['ssm_c_re'][l],
                                         p['ssm_c_im'][l], p['ssm_d'][l])
    head_avg = jnp.kron(jnp.eye(N_Q_HEADS, dtype=F32), jnp.full((HEAD_DIM, HEAD_DIM), 1.0 / HEAD_DIM, F32))
    wr = jnp.zeros((D_MODEL, LANES), F32)
    wr = wr.at[:, :N_EXPERTS].set(p['w_router'][l]).at[:, N_EXPERTS:N_EXPERTS + N_EXPERT_GROUPS].set(p['w_group'][l])
    br = jnp.zeros((1, LANES), F32)
    br = br.at[0, :N_EXPERTS].set(p['b_router'][l]).at[0, N_EXPERTS:N_EXPERTS + N_EXPERT_GROUPS].set(p['b_group'][l])
    return dict(
        w_in=w_in.astype(BF16),
        gq=(jnp.tile(p['q_norm_g'][l], N_Q_HEADS) * (HEAD_DIM ** -0.5)).reshape(1, ATTN_WIDTH),
        gk=jnp.tile(p['k_norm_g'][l], N_KV_HEADS).reshape(1, KV_WIDTH),
        bdq=head_avg.astype(BF16), bdk=head_avg[:KV_WIDTH, :KV_WIDTH].astype(BF16),
        m_t=m_t, in_t=in_t, o_t=o_t, tab=tab,
        wglu_t=p['w_glu'][l].T.astype(BF16), bglu=p['b_glu'][l].reshape(SSM_WIDTH, 1).astype(F32),
        wo=p['w_out'][l].astype(BF16),
        ga=p['attn_out_g'][l].reshape(1, -1), gs=p['ssm_out_g'][l].reshape(1, -1),
        ln1g=p['ln1_g'][l].reshape(1, -1), ln1b=p['ln1_b'][l].reshape(1, -1),
        wr=wr.astype(BF16), br=br,
        wgu=jnp.concatenate([p['w_gate'][l], p['w_up'][l]], axis=-1).astype(BF16),
        wd=p['w_down'][l].astype(BF16),
        ln2g=p['ln2_g'][l].reshape(1, -1), ln2b=p['ln2_b'][l].reshape(1, -1),
    )


def _trunk(x, p, weights):
    bsz, n_real, _ = x.shape
    tp = n_real + PREFIX
    meta = jnp.broadcast_to(p['meta_tokens'][None].astype(x.dtype), (bsz, N_META, D_MODEL))
    h = jnp.concatenate([jnp.zeros((bsz, N_PAD, D_MODEL), x.dtype), meta, x], axis=1)
    cos_t, sin_t = _rope_tables(n_real)
    n_chunks = tp // CHUNK
    ncp = -(-n_chunks // CHUNK_LANES) * CHUNK_LANES
    tm = _row_tile(tp, 1408)
    ln = (p['ln_in_g'].reshape(1, -1), p['ln_in_b'].reshape(1, -1))
    for l in range(DEPTH):
        w = weights[l]
        h, u4, q, k, v = _inproj(h, cos_t, sin_t, ln if l == 0 else None, w['w_in'], w['gq'], w['gk'],
                                 w['bdq'], w['bdk'], tm)
        attn = _attention(q, k, v)
        rt, sloc = _ssm_local(u4, w['in_t'], ncp)
        car = _ssm_scan(sloc, w['tab'], SCAN_LANES)
        ssm = _ssm_out(rt, car, w['m_t'], w['o_t'], w['wglu_t'], w['bglu'], tp)
        x1, gates = _outproj(attn, ssm, h, w['wo'], w['ga'], w['gs'], w['ln1g'], w['ln1b'],
                             w['wr'], w['br'], tm)
        h = _moe(x1, gates, w['wgu'], w['wd'], w['ln2g'], w['ln2b'], tm)
    return h[:, PREFIX:]


def kernel(x_prompt, x_sample, meta_tokens, ln_in_g, ln_in_b, w_in, q_norm_g, k_norm_g, ssm_lambda_re, ssm_lambda_im, ssm_log_dt, ssm_b_re, ssm_b_im, ssm_c_re, ssm_c_im, ssm_d, w_glu, b_glu, attn_out_g, ssm_out_g, w_out, ln1_g, ln1_b, w_group, b_group, w_router, b_router, w_gate, w_up, w_down, ln2_g, ln2_b):
    p = dict(meta_tokens=meta_tokens, ln_in_g=ln_in_g, ln_in_b=ln_in_b, w_in=w_in,
             q_norm_g=q_norm_g, k_norm_g=k_norm_g, ssm_lambda_re=ssm_lambda_re,
             ssm_lambda_im=ssm_lambda_im, ssm_log_dt=ssm_log_dt, ssm_b_re=ssm_b_re,
             ssm_b_im=ssm_b_im, ssm_c_re=ssm_c_re, ssm_c_im=ssm_c_im, ssm_d=ssm_d,
             w_glu=w_glu, b_glu=b_glu, attn_out_g=attn_out_g, ssm_out_g=ssm_out_g,
             w_out=w_out, ln1_g=ln1_g, ln1_b=ln1_b, w_group=w_group, b_group=b_group,
             w_router=w_router, b_router=b_router, w_gate=w_gate, w_up=w_up,
             w_down=w_down, ln2_g=ln2_g, ln2_b=ln2_b)
    weights = [_layer_weights(p, l) for l in range(DEPTH)]
    return (_trunk(x_prompt, p, weights), _trunk(x_sample, p, weights))
```

```python
import functools

import jax
import jax.numpy as jnp
from jax import lax
from jax.experimental import pallas as pl
from jax.experimental.pallas import tpu as pltpu

D_MODEL = 1024
N_META = 16
GRID_W = 64
ATTN_WIDTH = 512
SSM_WIDTH = 512
HEAD_DIM = 64
N_Q_HEADS = 8
N_KV_HEADS = 2
Q_PER_KV = 4
KV_WIDTH = 128
ROPE_THETA = 10000.0
SSM_GROUP = 16
N_SSM_GROUPS = 32
SSM_STATE = 64
N_EXPERT_GROUPS = 4
EXPERTS_PER_GROUP = 4
N_EXPERTS = 16
EXPERT_FF = 256
DEPTH = 2
DEEPNORM_ALPHA = (2.0 * DEPTH) ** 0.25
NORM_EPS = 1e-6

LANES = 128
SUBLANES = 8
PREFIX = 128
N_PAD = PREFIX - N_META
CHUNK = 16
CHUNK_W = CHUNK * SSM_GROUP
STATE_W = 4 * SSM_STATE
N_STATE_ROWS = N_SSM_GROUPS * SSM_STATE
CHUNK_LANES = 256
PAD_CHUNKS = N_PAD // CHUNK
SCAN_LANES = 512
MOE_ROWS = 704
Q_TILE = 128
K_TILE = 2048
VMEM_LIMIT = 56 * 1024 * 1024
SSM_OUT_VMEM_LIMIT = 60 * 1024 * 1024

F32 = jnp.float32
BF16 = jnp.bfloat16


def _cparams(sem, vmem_limit=VMEM_LIMIT):
    return pltpu.CompilerParams(dimension_semantics=sem, vmem_limit_bytes=vmem_limit)


def _layer_norm(x, g, b):
    mu = jnp.mean(x, axis=-1, keepdims=True)
    xc = x - mu
    var = jnp.mean(xc * xc, axis=-1, keepdims=True)
    return xc * lax.rsqrt(var + NORM_EPS) * g + b


def _swap_pairs(x):
    n = x.shape[-1]
    lane = lax.broadcasted_iota(jnp.int32, x.shape, x.ndim - 1)
    nxt = pltpu.roll(x, n - 1, x.ndim - 1)
    prv = pltpu.roll(x, 1, x.ndim - 1)
    return jnp.where((lane & 1) == 0, nxt, prv)


def _inproj_kernel(pre_ln, tm, *refs):
    if pre_ln:
        (x_ref, cos_ref, sin_ref, lng_ref, lnb_ref, w_ref, gq_ref, gk_ref, bdq_ref, bdk_ref,
         h_ref, u_ref, q_ref, k_ref, v_ref) = refs
    else:
        (x_ref, cos_ref, sin_ref, w_ref, gq_ref, gk_ref, bdq_ref, bdk_ref,
         u_ref, q_ref, k_ref, v_ref) = refs
    x = x_ref[0]
    if pre_ln:
        h = _layer_norm(x, lng_ref[...], lnb_ref[...])
        h_ref[0] = h
    else:
        h = x
    proj = jnp.dot(h.astype(BF16), w_ref[...], preferred_element_type=F32)
    q = proj[:, :ATTN_WIDTH]
    k = proj[:, ATTN_WIDTH:ATTN_WIDTH + KV_WIDTH]
    v = proj[:, ATTN_WIDTH + KV_WIDTH:ATTN_WIDTH + 2 * KV_WIDTH]
    for cb in range(SSM_WIDTH // LANES):
        base = ATTN_WIDTH + 2 * KV_WIDTH + cb * LANES
        u_ref[0, cb] = proj[:, base:base + LANES]
    cos = cos_ref[...]
    sin = sin_ref[...]

    q_ms = jnp.dot((q * q).astype(BF16), bdq_ref[...], preferred_element_type=F32)
    qn = q * lax.rsqrt(q_ms + NORM_EPS) * gq_ref[...]
    cos4 = jnp.concatenate([cos] * 4, axis=1)
    sin4 = jnp.concatenate([sin] * 4, axis=1)
    qr = qn * cos4 + _swap_pairs(qn) * sin4
    qr_odd = pltpu.roll(qr, ATTN_WIDTH - HEAD_DIM, 1)
    for hh in range(N_Q_HEADS):
        src = qr if hh % 2 == 0 else qr_odd
        base = (hh // 2) * LANES
        q_ref[0, hh] = src[:, base:base + HEAD_DIM].astype(BF16)

    k_ms = jnp.dot((k * k).astype(BF16), bdk_ref[...], preferred_element_type=F32)
    kn = k * lax.rsqrt(k_ms + NORM_EPS) * gk_ref[...]
    kr = kn * cos + _swap_pairs(kn) * sin
    k_ref[0, 0] = kr[:, :HEAD_DIM].astype(BF16)
    k_ref[0, 1] = pltpu.roll(kr, HEAD_DIM, 1)[:, :HEAD_DIM].astype(BF16)

    pos = pl.program_id(1) * tm + lax.broadcasted_iota(jnp.int32, (tm, 1), 0)
    valid = (pos >= N_PAD).astype(F32)
    lane = lax.broadcasted_iota(jnp.int32, (tm, KV_WIDTH), 1)
    ones_col = (lane == HEAD_DIM).astype(F32)
    v_sh = pltpu.roll(v, HEAD_DIM, 1)
    v_ref[0, 0] = (jnp.where(lane < HEAD_DIM, v, ones_col) * valid).astype(BF16)
    v_ref[0, 1] = (jnp.where(lane < HEAD_DIM, v_sh, ones_col) * valid).astype(BF16)


def _inproj(x, cos_t, sin_t, ln, w_in, gq, gk, bdq, bdk, tm):
    bsz, tp, _ = x.shape
    pre_ln = ln is not None
    grid = (bsz, tp // tm)
    row = lambda b, j: (b, j, 0)
    const2 = lambda b, j: (0, 0)
    in_specs = [pl.BlockSpec((1, tm, D_MODEL), row),
                pl.BlockSpec((tm, LANES), lambda b, j: (j, 0)),
                pl.BlockSpec((tm, LANES), lambda b, j: (j, 0))]
    args = [x, cos_t, sin_t]
    if pre_ln:
        in_specs += [pl.BlockSpec((1, D_MODEL), const2)] * 2
        args += [ln[0], ln[1]]
    in_specs += [pl.BlockSpec(w_in.shape, const2), pl.BlockSpec(gq.shape, const2),
                 pl.BlockSpec(gk.shape, const2), pl.BlockSpec(bdq.shape, const2),
                 pl.BlockSpec(bdk.shape, const2)]
    args += [w_in, gq, gk, bdq, bdk]
    head4 = lambda b, j: (b, 0, j, 0)
    out_shape = [jax.ShapeDtypeStruct((bsz, SSM_WIDTH // LANES, tp, LANES), F32),
                 jax.ShapeDtypeStruct((bsz, N_Q_HEADS, tp, HEAD_DIM), BF16),
                 jax.ShapeDtypeStruct((bsz, N_KV_HEADS, tp, HEAD_DIM), BF16),
                 jax.ShapeDtypeStruct((bsz, N_KV_HEADS, tp, LANES), BF16)]
    out_specs = [pl.BlockSpec((1, SSM_WIDTH // LANES, tm, LANES), head4),
                 pl.BlockSpec((1, N_Q_HEADS, tm, HEAD_DIM), head4),
                 pl.BlockSpec((1, N_KV_HEADS, tm, HEAD_DIM), head4),
                 pl.BlockSpec((1, N_KV_HEADS, tm, LANES), head4)]
    if pre_ln:
        out_shape = [jax.ShapeDtypeStruct((bsz, tp, D_MODEL), F32)] + out_shape
        out_specs = [pl.BlockSpec((1, tm, D_MODEL), row)] + out_specs
    outs = pl.pallas_call(
        functools.partial(_inproj_kernel, pre_ln, tm),
        grid=grid, in_specs=in_specs, out_specs=out_specs, out_shape=out_shape,
        compiler_params=_cparams(("parallel", "parallel")),
    )(*args)
    if pre_ln:
        return outs
    return [x] + list(outs)


def _attn_kernel(n_steps, q_ref, k_ref, v_ref, o_ref, m_sc, acc_sc):
    tq = q_ref.shape[2]
    rows = Q_PER_KV * tq
    nt = (((1,), (1,)), ((), ()))
    q = [q_ref[0, j * Q_PER_KV:(j + 1) * Q_PER_KV].reshape(rows, HEAD_DIM) for j in range(N_KV_HEADS)]

    def step(j, kt, vt, first):
        s = lax.dot_general(q[j], kt, nt, preferred_element_type=F32)
        m_cur = jnp.broadcast_to(jnp.max(s, axis=1, keepdims=True), (rows, LANES))
        m_new = m_cur if first else jnp.maximum(m_sc[j], m_cur)
        p = jnp.exp(s - jnp.tile(m_new, (1, s.shape[1] // LANES)))
        pv = jnp.dot(p.astype(BF16), vt, preferred_element_type=F32)
        if first:
            acc_sc[j] = pv
        else:
            acc_sc[j] = jnp.exp(m_sc[j] - m_new) * acc_sc[j] + pv
        m_sc[j] = m_new

    for j in range(N_KV_HEADS):
        step(j, k_ref[0, j, :PREFIX, :], v_ref[0, j, :PREFIX, :], True)

    def body(i, carry):
        off = pl.multiple_of(PREFIX + i * K_TILE, LANES)
        for j in range(N_KV_HEADS):
            step(j, k_ref[0, j, pl.ds(off, K_TILE), :], v_ref[0, j, pl.ds(off, K_TILE), :], False)
        return carry

    lax.fori_loop(0, n_steps, body, 0, unroll=4 if n_steps % 4 == 0 else 1)
    outs = []
    for j in range(N_KV_HEADS):
        acc = acc_sc[j]
        out = acc[:, :HEAD_DIM] / acc[:, HEAD_DIM:HEAD_DIM + 1]
        outs += [out[r * tq:(r + 1) * tq] for r in range(Q_PER_KV)]
    o_ref[0] = jnp.concatenate(outs, axis=1).astype(BF16)


def _attention(q, k, v):
    bsz, _, tp, _ = q.shape
    n_steps = (tp - PREFIX) // K_TILE
    rows = Q_PER_KV * Q_TILE
    once = pl.Buffered(1)
    return pl.pallas_call(
        functools.partial(_attn_kernel, n_steps),
        grid=(bsz, tp // Q_TILE),
        in_specs=[pl.BlockSpec((1, N_Q_HEADS, Q_TILE, HEAD_DIM), lambda b, i: (b, 0, i, 0)),
                  pl.BlockSpec((1, N_KV_HEADS, tp, HEAD_DIM), lambda b, i: (b, 0, 0, 0), pipeline_mode=once),
                  pl.BlockSpec((1, N_KV_HEADS, tp, LANES), lambda b, i: (b, 0, 0, 0), pipeline_mode=once)],
        out_specs=pl.BlockSpec((1, Q_TILE, ATTN_WIDTH), lambda b, i: (b, i, 0)),
        out_shape=jax.ShapeDtypeStruct((bsz, tp, ATTN_WIDTH), BF16),
        scratch_shapes=[pltpu.VMEM((N_KV_HEADS, rows, LANES), F32),
                        pltpu.VMEM((N_KV_HEADS, rows, LANES), F32)],
        compiler_params=_cparams(("parallel", "arbitrary")),
    )(q, k, v)


def _ssm_local_kernel(n_chunks, u_ref, in_ref, rt_ref, sloc_ref):
    wl = rt_ref.shape[2]
    col = pl.program_id(1) * wl + lax.broadcasted_iota(jnp.int32, (1, wl), 1)
    valid = (col >= PAD_CHUNKS) & (col < n_chunks)
    groups_per_block = LANES // SSM_GROUP
    for t in range(CHUNK):
        for cb in range(SSM_WIDTH // LANES):
            piece = u_ref[0, cb, pl.ds(t, wl, stride=CHUNK), :]
            piece = jnp.where(valid, piece.T, 0.0).astype(BF16)
            for gi in range(groups_per_block):
                g = cb * groups_per_block + gi
                rt_ref[0, g * CHUNK_W + t * SSM_GROUP:g * CHUNK_W + (t + 1) * SSM_GROUP, :] = (
                    piece[gi * SSM_GROUP:(gi + 1) * SSM_GROUP, :])
    ns = SSM_STATE
    for m in range(N_SSM_GROUPS // 2):
        sl = [jnp.dot(in_ref[g], rt_ref[0, g * CHUNK_W:(g + 1) * CHUNK_W, :], preferred_element_type=F32)
              for g in (2 * m, 2 * m + 1)]
        for d in range(2):
            pair = jnp.concatenate([sl[0][2 * d * ns:(2 * d + 1) * ns], sl[1][2 * d * ns:(2 * d + 1) * ns],
                                    sl[0][(2 * d + 1) * ns:(2 * d + 2) * ns],
                                    sl[1][(2 * d + 1) * ns:(2 * d + 2) * ns]], axis=0).T
            sloc_ref[0, 2 * d, :, m * LANES:(m + 1) * LANES] = pair[:, :LANES]
            sloc_ref[0, 2 * d + 1, :, m * LANES:(m + 1) * LANES] = pair[:, LANES:]


def _ssm_local(u4, in_t, ncp):
    bsz, _, tp, _ = u4.shape
    n_chunks = tp // CHUNK
    wl = CHUNK_LANES
    return pl.pallas_call(
        functools.partial(_ssm_local_kernel, n_chunks),
        grid=(bsz, ncp // wl),
        in_specs=[pl.BlockSpec((1, SSM_WIDTH // LANES, wl * CHUNK, LANES), lambda b, c: (b, 0, c, 0)),
                  pl.BlockSpec(in_t.shape, lambda b, c: (0, 0, 0))],
        out_specs=[pl.BlockSpec((1, N_SSM_GROUPS * CHUNK_W, wl), lambda b, c: (b, 0, c)),
                   pl.BlockSpec((1, 4, wl, N_STATE_ROWS), lambda b, c: (b, 0, c, 0))],
        out_shape=[jax.ShapeDtypeStruct((bsz, N_SSM_GROUPS * CHUNK_W, ncp), BF16),
                   jax.ShapeDtypeStruct((bsz, 4, ncp, N_STATE_ROWS), F32)],
        compiler_params=_cparams(("parallel", "parallel")),
    )(u4, in_t)


def _ssm_scan_kernel(sloc_ref, tab_ref, car_ref, st_sc):
    ncp, lb = sloc_ref.shape[2], sloc_ref.shape[3]
    n_groups = ncp // SUBLANES
    row = lax.broadcasted_iota(jnp.int32, (ncp, 1), 0)
    sub = row & (SUBLANES - 1)

    for c0, reverse in ((0, False), (2, True)):
        re, im = sloc_ref[0, c0], sloc_ref[0, c0 + 1]
        for s in range(3):
            d = 1 << s
            keep = (sub < SUBLANES - d) if reverse else (sub >= d)
            shift = ncp - d if reverse else d
            r_re = jnp.where(keep, pltpu.roll(re, shift, 0), 0.0)
            r_im = jnp.where(keep, pltpu.roll(im, shift, 0), 0.0)
            a_re = tab_ref[c0, SUBLANES + s:SUBLANES + s + 1, :]
            a_im = tab_ref[c0 + 1, SUBLANES + s:SUBLANES + s + 1, :]
            re, im = re + (a_re * r_re - a_im * r_im), im + (a_re * r_im + a_im * r_re)
        st_sc[0] = re
        st_sc[1] = im
        p_re, p_im = tab_ref[c0, :SUBLANES, :], tab_ref[c0 + 1, :SUBLANES, :]
        edge = 0 if reverse else SUBLANES - 1

        def body(i, carry):
            c_re, c_im = carry
            grp = n_groups - 1 - i if reverse else i
            off = pl.multiple_of(grp * SUBLANES, SUBLANES)
            x_re = st_sc[0, pl.ds(off, SUBLANES), :] + (p_re * c_re - p_im * c_im)
            x_im = st_sc[1, pl.ds(off, SUBLANES), :] + (p_re * c_im + p_im * c_re)
            st_sc[0, pl.ds(off, SUBLANES), :] = x_re
            st_sc[1, pl.ds(off, SUBLANES), :] = x_im
            return x_re[edge:edge + 1], x_im[edge:edge + 1]

        zero = jnp.zeros((1, lb), F32)
        lax.fori_loop(0, n_groups, body, (zero, zero))
        keep = (row < ncp - 1) if reverse else (row >= 1)
        shift = ncp - 1 if reverse else 1
        car_ref[0, c0] = jnp.where(keep, pltpu.roll(st_sc[0], shift, 0), 0.0).astype(BF16)
        car_ref[0, c0 + 1] = jnp.where(keep, pltpu.roll(st_sc[1], shift, 0), 0.0).astype(BF16)


def _ssm_scan(sloc, tab, lb):
    bsz, _, ncp, _ = sloc.shape
    return pl.pallas_call(
        _ssm_scan_kernel,
        grid=(bsz, N_STATE_ROWS // lb),
        in_specs=[pl.BlockSpec((1, 4, ncp, lb), lambda b, r: (b, 0, 0, r)),
                  pl.BlockSpec((4, 2 * SUBLANES, lb), lambda b, r: (0, 0, r))],
        out_specs=pl.BlockSpec((1, 4, ncp, lb), lambda b, r: (b, 0, 0, r)),
        out_shape=jax.ShapeDtypeStruct((bsz, 4, ncp, N_STATE_ROWS), BF16),
        scratch_shapes=[pltpu.VMEM((2, ncp, lb), F32)],
        compiler_params=_cparams(("parallel", "parallel")),
    )(sloc, tab)


def _ssm_out_kernel(rt_ref, car_ref, m_ref, o_ref, wg_ref, bg_ref, out_ref, y_sc):
    wl = rt_ref.shape[2]
    nt = (((1,), (1,)), ((), ()))
    for m in range(N_SSM_GROUPS // 2):
        sc = jnp.concatenate([car_ref[0, c, :, m * LANES:(m + 1) * LANES] for c in range(4)], axis=1)
        y_car = lax.dot_general(o_ref[m], sc, nt, preferred_element_type=F32)
        for i in range(2):
            g = 2 * m + i
            rg = rt_ref[0, g * CHUNK_W:(g + 1) * CHUNK_W, :]
            y = jnp.dot(m_ref[g], rg, preferred_element_type=F32) + y_car[i * CHUNK_W:(i + 1) * CHUNK_W]
            y = jax.nn.gelu(y).astype(y_sc.dtype)
            for t in range(CHUNK):
                y_sc[t * SSM_WIDTH + g * SSM_GROUP:t * SSM_WIDTH + (g + 1) * SSM_GROUP, :] = (
                    y[t * SSM_GROUP:(t + 1) * SSM_GROUP, :])
    for t in range(CHUNK):
        slab = y_sc[t * SSM_WIDTH:(t + 1) * SSM_WIDTH, :]
        z = jnp.dot(wg_ref[...], slab, preferred_element_type=F32) + bg_ref[...]
        o = slab.astype(F32) * jax.nn.sigmoid(z)
        o_t = o.T
        for cb in range(SSM_WIDTH // LANES):
            out_ref[0, cb, pl.ds(t, wl, stride=CHUNK), :] = o_t[:, cb * LANES:(cb + 1) * LANES]


def _ssm_out(rt, car, m_t, o_t, wglu_t, bglu, tp):
    bsz, _, ncp = rt.shape
    wl = CHUNK_LANES
    once = pl.Buffered(1)
    return pl.pallas_call(
        _ssm_out_kernel,
        grid=(bsz, ncp // wl),
        in_specs=[pl.BlockSpec((1, N_SSM_GROUPS * CHUNK_W, wl), lambda b, c: (b, 0, c)),
                  pl.BlockSpec((1, 4, wl, N_STATE_ROWS), lambda b, c: (b, 0, c, 0)),
                  pl.BlockSpec(m_t.shape, lambda b, c: (0, 0, 0), pipeline_mode=once),
                  pl.BlockSpec(o_t.shape, lambda b, c: (0, 0, 0), pipeline_mode=once),
                  pl.BlockSpec(wglu_t.shape, lambda b, c: (0, 0)),
                  pl.BlockSpec(bglu.shape, lambda b, c: (0, 0))],
        out_specs=pl.BlockSpec((1, SSM_WIDTH // LANES, wl * CHUNK, LANES), lambda b, c: (b, 0, c, 0)),
        out_shape=jax.ShapeDtypeStruct((bsz, SSM_WIDTH // LANES, tp, LANES), F32),
        scratch_shapes=[pltpu.VMEM((CHUNK * SSM_WIDTH, wl), BF16)],
        compiler_params=_cparams(("parallel", "parallel"), SSM_OUT_VMEM_LIMIT),
    )(rt, car, m_t, o_t, wglu_t, bglu)


def _ssm_operators(lam_re, lam_im, log_dt, b_re, b_im, c_re, c_im, d_skip):
    g_n, p_n, h_n = N_SSM_GROUPS, SSM_STATE, SSM_GROUP
    hi = lax.Precision.HIGHEST
    kern, w_in, w_out, tab = [], [], [], []
    fwd_pow = jnp.arange(1, SUBLANES + 1, dtype=F32)
    level1_pow = jnp.array([1.0, 2.0, 4.0] + [0.0] * (SUBLANES - 3), F32)
    for direction in range(2):
        l_re, l_im = lam_re[direction].astype(F32), lam_im[direction].astype(F32)
        dt = jnp.exp(log_dt[direction].astype(F32))[:, None]
        ldt_re, ldt_im = l_re * dt, l_im * dt
        j = jnp.arange(CHUNK + 1, dtype=F32)[:, None, None]
        mag = jnp.exp(j * ldt_re[None])
        p_re, p_im = mag * jnp.cos(j * ldt_im[None]), mag * jnp.sin(j * ldt_im[None])
        a, b = p_re[1] - 1.0, p_im[1]
        den = l_re * l_re + l_im * l_im
        z_re, z_im = (a * l_re + b * l_im) / den, (b * l_re - a * l_im) / den
        br, bi = b_re[direction].astype(F32), b_im[direction].astype(F32)
        bb_re = z_re[:, :, None] * br - z_im[:, :, None] * bi
        bb_im = z_re[:, :, None] * bi + z_im[:, :, None] * br
        w_re = p_re[:, :, :, None] * bb_re[None] - p_im[:, :, :, None] * bb_im[None]
        w_im = p_re[:, :, :, None] * bb_im[None] + p_im[:, :, :, None] * bb_re[None]
        cr, ci = c_re[direction].astype(F32), c_im[direction].astype(F32)
        kern.append(jnp.einsum('ghp,jgpk->jghk', cr, w_re[:CHUNK], precision=hi)
                    - jnp.einsum('ghp,jgpk->jghk', ci, w_im[:CHUNK], precision=hi))
        w_in.append((w_re, w_im))
        w_out.append((cr[None] * p_re[:, :, None, :] - ci[None] * p_im[:, :, None, :],
                      cr[None] * p_im[:, :, None, :] + ci[None] * p_re[:, :, None, :]))
        scan_pow = CHUNK * jnp.concatenate([fwd_pow if direction == 0 else fwd_pow[::-1], level1_pow])
        d_mag = jnp.exp(scan_pow[:, None] * ldt_re.reshape(1, -1))
        d_ang = scan_pow[:, None] * ldt_im.reshape(1, -1)
        tab += [d_mag * jnp.cos(d_ang), d_mag * jnp.sin(d_ang)]
    t_idx = jnp.arange(CHUNK)
    diff = t_idx[:, None] - t_idx[None, :]
    kf = jnp.where((diff >= 0)[:, :, None, None, None], kern[0][jnp.clip(diff, 0, CHUNK - 1)], 0.0)
    kb = jnp.where((diff <= 0)[:, :, None, None, None], kern[1][jnp.clip(-diff, 0, CHUNK - 1)], 0.0)
    skip = (jnp.eye(CHUNK, dtype=F32)[:, :, None, None, None]
            * (jnp.eye(h_n, dtype=F32)[None] * d_skip.astype(F32).reshape(g_n, 1, h_n))[None, None])
    m_t = (kf + kb + skip).transpose(2, 0, 3, 1, 4).reshape(g_n, CHUNK_W, CHUNK_W)
    in_parts = []
    for direction in range(2):
        for part in w_in[direction]:
            sel = part[:CHUNK][::-1] if direction == 0 else part[:CHUNK]
            in_parts.append(sel.transpose(1, 2, 0, 3).reshape(g_n, p_n, CHUNK_W))
    in_t = jnp.concatenate(in_parts, axis=1)
    out_parts = []
    for direction in range(2):
        for sign, part in zip((1.0, -1.0), w_out[direction]):
            sel = part[1:] if direction == 0 else part[1:][::-1]
            out_parts.append(sign * sel.transpose(1, 0, 2, 3).reshape(g_n, CHUNK_W, p_n))
    eye2 = jnp.eye(2, dtype=F32)
    o_pair = jnp.stack([part.reshape(g_n // 2, 2, CHUNK_W, p_n) for part in out_parts], axis=3)
    o_pair = (o_pair[:, :, :, :, None, :] * eye2[None, :, None, None, :, None]).reshape(
        g_n // 2, 2 * CHUNK_W, 4 * 2 * p_n)
    return m_t.astype(BF16), in_t.astype(BF16), o_pair.astype(BF16), jnp.stack(tab)


def _outproj_kernel(attn_ref, ssm_ref, h_ref, wo_ref, ga_ref, gs_ref, lg_ref, lb_ref,
                    wr_ref, br_ref, x1_ref, gate_ref):
    a = attn_ref[0].astype(F32)
    s = jnp.concatenate([ssm_ref[0, cb] for cb in range(SSM_WIDTH // LANES)], axis=1)
    an = a * lax.rsqrt(jnp.mean(a * a, axis=-1, keepdims=True) + NORM_EPS) * ga_ref[...]
    sn = s * lax.rsqrt(jnp.mean(s * s, axis=-1, keepdims=True) + NORM_EPS) * gs_ref[...]
    mixed = (jnp.dot(an.astype(BF16), wo_ref[:ATTN_WIDTH, :], preferred_element_type=F32)
             + jnp.dot(sn.astype(BF16), wo_ref[ATTN_WIDTH:, :], preferred_element_type=F32))
    x1 = _layer_norm(DEEPNORM_ALPHA * h_ref[0] + mixed, lg_ref[...], lb_ref[...])
    x1_ref[0] = x1

    logits = jnp.dot(x1.astype(BF16), wr_ref[...], preferred_element_type=F32) + br_ref[...]
    lane = lax.broadcasted_iota(jnp.int32, logits.shape, 1)
    neg = -jnp.inf
    is_group = (lane >= N_EXPERTS) & (lane < N_EXPERTS + N_EXPERT_GROUPS)
    glog = jnp.where(is_group, logits, neg)
    gmax = jnp.max(glog, axis=1, keepdims=True)
    g_val = 1.0 / jnp.sum(jnp.exp(glog - gmax), axis=1, keepdims=True)
    g_idx = jnp.min(jnp.where(glog == gmax, lane, LANES), axis=1, keepdims=True) - N_EXPERTS
    in_group = (lane < N_EXPERTS) & ((lane >> 2) == g_idx)
    el = jnp.where(in_group, logits, neg)
    v1 = jnp.max(el, axis=1, keepdims=True)
    i1 = jnp.min(jnp.where(el == v1, lane, LANES), axis=1, keepdims=True)
    el2 = jnp.where(lane == i1, neg, el)
    v2 = jnp.max(el2, axis=1, keepdims=True)
    i2 = jnp.min(jnp.where(el2 == v2, lane, LANES), axis=1, keepdims=True)
    e2 = jnp.exp(v2 - v1)
    w1 = g_val / (1.0 + e2)
    gate_ref[0] = jnp.where(lane == i1, w1, jnp.where(lane == i2, w1 * e2, 0.0))


def _outproj(attn, ssm, h, wo, ga, gs, lg, lb, wr, br, tm):
    bsz, tp, _ = h.shape
    row = lambda b, j: (b, j, 0)
    c2 = lambda b, j: (0, 0)
    return pl.pallas_call(
        _outproj_kernel,
        grid=(bsz, tp // tm),
        in_specs=[pl.BlockSpec((1, tm, ATTN_WIDTH), row),
                  pl.BlockSpec((1, SSM_WIDTH // LANES, tm, LANES), lambda b, j: (b, 0, j, 0)),
                  pl.BlockSpec((1, tm, D_MODEL), row), pl.BlockSpec(wo.shape, c2),
                  pl.BlockSpec(ga.shape, c2), pl.BlockSpec(gs.shape, c2),
                  pl.BlockSpec(lg.shape, c2), pl.BlockSpec(lb.shape, c2),
                  pl.BlockSpec(wr.shape, c2), pl.BlockSpec(br.shape, c2)],
        out_specs=[pl.BlockSpec((1, tm, D_MODEL), row), pl.BlockSpec((1, tm, LANES), row)],
        out_shape=[jax.ShapeDtypeStruct((bsz, tp, D_MODEL), F32),
                   jax.ShapeDtypeStruct((bsz, tp, LANES), F32)],
        compiler_params=_cparams(("parallel", "parallel")),
    )(attn, ssm, h, wo, ga, gs, lg, lb, wr, br)


def _moe_kernel(x_ref, gate_ref, wgu_ref, wd_ref, lg_ref, lb_ref, y_ref, hh_sc):
    x = x_ref[0]
    xb = x.astype(BF16)
    gates = gate_ref[0]
    for e in range(N_EXPERTS):
        hgu = jnp.dot(xb, wgu_ref[e], preferred_element_type=F32)
        hh = jax.nn.silu(hgu[:, :EXPERT_FF]) * hgu[:, EXPERT_FF:] * gates[:, e:e + 1]
        hh_sc[:, e * EXPERT_FF:(e + 1) * EXPERT_FF] = hh.astype(BF16)
    moe = jnp.dot(hh_sc[...], wd_ref[...], preferred_element_type=F32)
    y_ref[0] = _layer_norm(DEEPNORM_ALPHA * x + moe, lg_ref[...], lb_ref[...])


def _moe(x1, gates, wgu, wd, lg, lb, tm):
    bsz, tp, _ = x1.shape
    row = lambda b, j: (b, j, 0)
    c2 = lambda b, j: (0, 0)
    once = pl.Buffered(1)
    return pl.pallas_call(
        _moe_kernel,
        grid=(bsz, tp // tm),
        in_specs=[pl.BlockSpec((1, tm, D_MODEL), row), pl.BlockSpec((1, tm, LANES), row),
                  pl.BlockSpec(wgu.shape, lambda b, j: (0, 0, 0), pipeline_mode=once),
                  pl.BlockSpec(wd.shape, c2, pipeline_mode=once),
                  pl.BlockSpec(lg.shape, c2), pl.BlockSpec(lb.shape, c2)],
        out_specs=pl.BlockSpec((1, tm, D_MODEL), row),
        out_shape=jax.ShapeDtypeStruct((bsz, tp, D_MODEL), F32),
        scratch_shapes=[pltpu.VMEM((tm, N_EXPERTS * EXPERT_FF), BF16)],
        compiler_params=_cparams(("parallel", "parallel")),
    )(x1, gates, wgu, wd, lg, lb)


def _rope_tables(n_real):
    r = jnp.arange(n_real, dtype=jnp.int32)
    row = jnp.concatenate([jnp.zeros((PREFIX,), F32), (r // GRID_W).astype(F32)])
    col = jnp.concatenate([jnp.zeros((PREFIX,), F32), (r % GRID_W).astype(F32)])
    n_freq = HEAD_DIM // 4
    inv_freq = ROPE_THETA ** (-jnp.arange(n_freq, dtype=F32) / n_freq)
    ang = jnp.concatenate([row[:, None] * inv_freq, col[:, None] * inv_freq], axis=-1)
    cos = jnp.repeat(jnp.cos(ang), 2, axis=-1)
    sin = jnp.repeat(jnp.sin(ang), 2, axis=-1) * jnp.tile(jnp.array([-1.0, 1.0], F32), HEAD_DIM // 2)
    return jnp.tile(cos, (1, 2)), jnp.tile(sin, (1, 2))


def _row_tile(tp, cap):
    best = 8
    for t in range(8, cap + 1, 8):
        if tp % t == 0:
            best = t
    return best


def _layer_weights(p, l):
    w_in = p['w_in'][l]
    m_t, in_t, o_t, tab = _ssm_operators(p['ssm_lambda_re'][l], p['ssm_lambda_im'][l], p['ssm_log_dt'][l],
                                         p['ssm_b_re'][l], p['ssm_b_im'][l], p['ssm_c_re'][l],
                                         p['ssm_c_im'][l], p['ssm_d'][l])
    head_avg = jnp.kron(jnp.eye(N_Q_HEADS, dtype=F32), jnp.full((HEAD_DIM, HEAD_DIM), 1.0 / HEAD_DIM, F32))
    wr = jnp.zeros((D_MODEL, LANES), F32)
    wr = wr.at[:, :N_EXPERTS].set(p['w_router'][l]).at[:, N_EXPERTS:N_EXPERTS + N_EXPERT_GROUPS].set(p['w_group'][l])
    br = jnp.zeros((1, LANES), F32)
    br = br.at[0, :N_EXPERTS].set(p['b_router'][l]).at[0, N_EXPERTS:N_EXPERTS + N_EXPERT_GROUPS].set(p['b_group'][l])
    return dict(
        w_in=w_in.astype(BF16),
        gq=(jnp.tile(p['q_norm_g'][l], N_Q_HEADS) * (HEAD_DIM ** -0.5)).reshape(1, ATTN_WIDTH),
        gk=jnp.tile(p['k_norm_g'][l], N_KV_HEADS).reshape(1, KV_WIDTH),
        bdq=head_avg.astype(BF16), bdk=head_avg[:KV_WIDTH, :KV_WIDTH].astype(BF16),
        m_t=m_t, in_t=in_t, o_t=o_t, tab=tab,
        wglu_t=p['w_glu'][l].T.astype(BF16), bglu=p['b_glu'][l].reshape(SSM_WIDTH, 1).astype(F32),
        wo=p['w_out'][l].astype(BF16),
        ga=p['attn_out_g'][l].reshape(1, -1), gs=p['ssm_out_g'][l].reshape(1, -1),
        ln1g=p['ln1_g'][l].reshape(1, -1), ln1b=p['ln1_b'][l].reshape(1, -1),
        wr=wr.astype(BF16), br=br,
        wgu=jnp.concatenate([p['w_gate'][l], p['w_up'][l]], axis=-1).astype(BF16),
        wd=p['w_down'][l].reshape(N_EXPERTS * EXPERT_FF, D_MODEL).astype(BF16),
        ln2g=p['ln2_g'][l].reshape(1, -1), ln2b=p['ln2_b'][l].reshape(1, -1),
    )


def _trunk(x, p, weights):
    bsz, n_real, _ = x.shape
    tp = n_real + PREFIX
    meta = jnp.broadcast_to(p['meta_tokens'][None].astype(x.dtype), (bsz, N_META, D_MODEL))
    h = jnp.concatenate([jnp.zeros((bsz, N_PAD, D_MODEL), x.dtype), meta, x], axis=1)
    cos_t, sin_t = _rope_tables(n_real)
    n_chunks = tp // CHUNK
    ncp = -(-n_chunks // CHUNK_LANES) * CHUNK_LANES
    tm = _row_tile(tp, 1408)
    ln = (p['ln_in_g'].reshape(1, -1), p['ln_in_b'].reshape(1, -1))
    for l in range(DEPTH):
        w = weights[l]
        h, u4, q, k, v = _inproj(h, cos_t, sin_t, ln if l == 0 else None, w['w_in'], w['gq'], w['gk'],
                                 w['bdq'], w['bdk'], tm)
        attn = _attention(q, k, v)
        rt, sloc = _ssm_local(u4, w['in_t'], ncp)
        car = _ssm_scan(sloc, w['tab'], SCAN_LANES)
        ssm = _ssm_out(rt, car, w['m_t'], w['o_t'], w['wglu_t'], w['bglu'], tp)
        x1, gates = _outproj(attn, ssm, h, w['wo'], w['ga'], w['gs'], w['ln1g'], w['ln1b'],
                             w['wr'], w['br'], tm)
        h = _moe(x1, gates, w['wgu'], w['wd'], w['ln2g'], w['ln2b'], _row_tile(tp, MOE_ROWS))
    return h[:, PREFIX:]


def kernel(x_prompt, x_sample, meta_tokens, ln_in_g, ln_in_b, w_in, q_norm_g, k_norm_g, ssm_lambda_re, ssm_lambda_im, ssm_log_dt, ssm_b_re, ssm_b_im, ssm_c_re, ssm_c_im, ssm_d, w_glu, b_glu, attn_out_g, ssm_out_g, w_out, ln1_g, ln1_b, w_group, b_group, w_router, b_router, w_gate, w_up, w_down, ln2_g, ln2_b):
    p = dict(meta_tokens=meta_tokens, ln_in_g=ln_in_g, ln_in_b=ln_in_b, w_in=w_in,
             q_norm_g=q_norm_g, k_norm_g=k_norm_g, ssm_lambda_re=ssm_lambda_re,
             ssm_lambda_im=ssm_lambda_im, ssm_log_dt=ssm_log_dt, ssm_b_re=ssm_b_re,
             ssm_b_im=ssm_b_im, ssm_c_re=ssm_c_re, ssm_c_im=ssm_c_im, ssm_d=ssm_d,
             w_glu=w_glu, b_glu=b_glu, attn_out_g=attn_out_g, ssm_out_g=ssm_out_g,
             w_out=w_out, ln1_g=ln1_g, ln1_b=ln1_b, w_group=w_group, b_group=b_group,
             w_router=w_router, b_router=b_router, w_gate=w_gate, w_up=w_up,
             w_down=w_down, ln2_g=ln2_g, ln2_b=ln2_b)
    weights = [_layer_weights(p, l) for l in range(DEPTH)]
    return (_trunk(x_prompt, p, weights), _trunk(x_sample, p, weights))
```

```python
import functools

import jax
import jax.numpy as jnp
from jax import lax
from jax.experimental import pallas as pl
from jax.experimental.pallas import tpu as pltpu

D_MODEL = 1024
N_META = 16
GRID_W = 64
ATTN_WIDTH = 512
SSM_WIDTH = 512
HEAD_DIM = 64
N_Q_HEADS = 8
N_KV_HEADS = 2
Q_PER_KV = 4
KV_WIDTH = 128
ROPE_THETA = 10000.0
SSM_GROUP = 16
N_SSM_GROUPS = 32
SSM_STATE = 64
N_EXPERT_GROUPS = 4
EXPERTS_PER_GROUP = 4
N_EXPERTS = 16
EXPERT_FF = 256
DEPTH = 2
DEEPNORM_ALPHA = (2.0 * DEPTH) ** 0.25
NORM_EPS = 1e-6

LANES = 128
SUBLANES = 8
PREFIX = 128
N_PAD = PREFIX - N_META
CHUNK = 16
CHUNK_W = CHUNK * SSM_GROUP
STATE_W = 4 * SSM_STATE
N_STATE_ROWS = N_SSM_GROUPS * SSM_STATE
CHUNK_LANES = 256
PAD_CHUNKS = N_PAD // CHUNK
SCAN_LANES = 512
MOE_ROWS = 704
Q_TILE = 128
K_TILE = 2048
VMEM_LIMIT = 56 * 1024 * 1024
SSM_OUT_VMEM_LIMIT = 60 * 1024 * 1024

F32 = jnp.float32
BF16 = jnp.bfloat16


def _cparams(sem, vmem_limit=VMEM_LIMIT):
    return pltpu.CompilerParams(dimension_semantics=sem, vmem_limit_bytes=vmem_limit)


def _layer_norm(x, g, b):
    mu = jnp.mean(x, axis=-1, keepdims=True)
    xc = x - mu
    var = jnp.mean(xc * xc, axis=-1, keepdims=True)
    return xc * lax.rsqrt(var + NORM_EPS) * g + b


def _swap_pairs(x):
    n = x.shape[-1]
    lane = lax.broadcasted_iota(jnp.int32, x.shape, x.ndim - 1)
    nxt = pltpu.roll(x, n - 1, x.ndim - 1)
    prv = pltpu.roll(x, 1, x.ndim - 1)
    return jnp.where((lane & 1) == 0, nxt, prv)


def _inproj_kernel(pre_ln, tm, *refs):
    if pre_ln:
        (x_ref, cos_ref, sin_ref, lng_ref, lnb_ref, w_ref, gq_ref, gk_ref, bdq_ref, bdk_ref,
         h_ref, u_ref, q_ref, k_ref, v_ref) = refs
    else:
        (x_ref, cos_ref, sin_ref, w_ref, gq_ref, gk_ref, bdq_ref, bdk_ref,
         u_ref, q_ref, k_ref, v_ref) = refs
    x = x_ref[0]
    if pre_ln:
        h = _layer_norm(x, lng_ref[...], lnb_ref[...])
        h_ref[0] = h
    else:
        h = x
    proj = jnp.dot(h.astype(BF16), w_ref[...], preferred_element_type=F32)
    q = proj[:, :ATTN_WIDTH]
    k = proj[:, ATTN_WIDTH:ATTN_WIDTH + KV_WIDTH]
    v = proj[:, ATTN_WIDTH + KV_WIDTH:ATTN_WIDTH + 2 * KV_WIDTH]
    for cb in range(SSM_WIDTH // LANES):
        base = ATTN_WIDTH + 2 * KV_WIDTH + cb * LANES
        u_ref[0, cb] = proj[:, base:base + LANES]
    cos = cos_ref[...]
    sin = sin_ref[...]

    q_ms = jnp.dot((q * q).astype(BF16), bdq_ref[...], preferred_element_type=F32)
    qn = q * lax.rsqrt(q_ms + NORM_EPS) * gq_ref[...]
    cos4 = jnp.concatenate([cos] * 4, axis=1)
    sin4 = jnp.concatenate([sin] * 4, axis=1)
    qr = qn * cos4 + _swap_pairs(qn) * sin4
    qr_odd = pltpu.roll(qr, ATTN_WIDTH - HEAD_DIM, 1)
    for hh in range(N_Q_HEADS):
        src = qr if hh % 2 == 0 else qr_odd
        base = (hh // 2) * LANES
        q_ref[0, hh] = src[:, base:base + HEAD_DIM].astype(BF16)

    k_ms = jnp.dot((k * k).astype(BF16), bdk_ref[...], preferred_element_type=F32)
    kn = k * lax.rsqrt(k_ms + NORM_EPS) * gk_ref[...]
    kr = kn * cos + _swap_pairs(kn) * sin
    k_ref[0, 0] = kr[:, :HEAD_DIM].astype(BF16)
    k_ref[0, 1] = pltpu.roll(kr, HEAD_DIM, 1)[:, :HEAD_DIM].astype(BF16)

    pos = pl.program_id(1) * tm + lax.broadcasted_iota(jnp.int32, (tm, 1), 0)
    valid = (pos >= N_PAD).astype(F32)
    lane = lax.broadcasted_iota(jnp.int32, (tm, KV_WIDTH), 1)
    ones_col = (lane == HEAD_DIM).astype(F32)
    v_sh = pltpu.roll(v, HEAD_DIM, 1)
    v_ref[0, 0] = (jnp.where(lane < HEAD_DIM, v, ones_col) * valid).astype(BF16)
    v_ref[0, 1] = (jnp.where(lane < HEAD_DIM, v_sh, ones_col) * valid).astype(BF16)


def _inproj(x, cos_t, sin_t, ln, w_in, gq, gk, bdq, bdk, tm):
    bsz, tp, _ = x.shape
    pre_ln = ln is not None
    grid = (bsz, tp // tm)
    row = lambda b, j: (b, j, 0)
    const2 = lambda b, j: (0, 0)
    in_specs = [pl.BlockSpec((1, tm, D_MODEL), row),
                pl.BlockSpec((tm, LANES), lambda b, j: (j, 0)),
                pl.BlockSpec((tm, LANES), lambda b, j: (j, 0))]
    args = [x, cos_t, sin_t]
    if pre_ln:
        in_specs += [pl.BlockSpec((1, D_MODEL), const2)] * 2
        args += [ln[0], ln[1]]
    in_specs += [pl.BlockSpec(w_in.shape, const2), pl.BlockSpec(gq.shape, const2),
                 pl.BlockSpec(gk.shape, const2), pl.BlockSpec(bdq.shape, const2),
                 pl.BlockSpec(bdk.shape, const2)]
    args += [w_in, gq, gk, bdq, bdk]
    head4 = lambda b, j: (b, 0, j, 0)
    out_shape = [jax.ShapeDtypeStruct((bsz, SSM_WIDTH // LANES, tp, LANES), F32),
                 jax.ShapeDtypeStruct((bsz, N_Q_HEADS, tp, HEAD_DIM), BF16),
                 jax.ShapeDtypeStruct((bsz, N_KV_HEADS, tp, HEAD_DIM), BF16),
                 jax.ShapeDtypeStruct((bsz, N_KV_HEADS, tp, LANES), BF16)]
    out_specs = [pl.BlockSpec((1, SSM_WIDTH // LANES, tm, LANES), head4),
                 pl.BlockSpec((1, N_Q_HEADS, tm, HEAD_DIM), head4),
                 pl.BlockSpec((1, N_KV_HEADS, tm, HEAD_DIM), head4),
                 pl.BlockSpec((1, N_KV_HEADS, tm, LANES), head4)]
    if pre_ln:
        out_shape = [jax.ShapeDtypeStruct((bsz, tp, D_MODEL), F32)] + out_shape
        out_specs = [pl.BlockSpec((1, tm, D_MODEL), row)] + out_specs
    outs = pl.pallas_call(
        functools.partial(_inproj_kernel, pre_ln, tm),
        grid=grid, in_specs=in_specs, out_specs=out_specs, out_shape=out_shape,
        compiler_params=_cparams(("parallel", "parallel")),
    )(*args)
    if pre_ln:
        return outs
    return [x] + list(outs)


def _attn_kernel(n_steps, q_ref, k_ref, v_ref, o_ref, m_sc, acc_sc):
    tq = q_ref.shape[2]
    rows = Q_PER_KV * tq
    nt = (((1,), (1,)), ((), ()))
    q = [q_ref[0, j * Q_PER_KV:(j + 1) * Q_PER_KV].reshape(rows, HEAD_DIM) for j in range(N_KV_HEADS)]

    def step(j, kt, vt, first):
        s = lax.dot_general(q[j], kt, nt, preferred_element_type=F32)
        m_cur = jnp.broadcast_to(jnp.max(s, axis=1, keepdims=True), (rows, LANES))
        m_new = m_cur if first else jnp.maximum(m_sc[j], m_cur)
        p = jnp.exp(s - jnp.tile(m_new, (1, s.shape[1] // LANES)))
        pv = jnp.dot(p.astype(BF16), vt, preferred_element_type=F32)
        if first:
            acc_sc[j] = pv
        else:
            acc_sc[j] = jnp.exp(m_sc[j] - m_new) * acc_sc[j] + pv
        m_sc[j] = m_new

    for j in range(N_KV_HEADS):
        step(j, k_ref[0, j, :PREFIX, :], v_ref[0, j, :PREFIX, :], True)

    def body(i, carry):
        off = pl.multiple_of(PREFIX + i * K_TILE, LANES)
        for j in range(N_KV_HEADS):
            step(j, k_ref[0, j, pl.ds(off, K_TILE), :], v_ref[0, j, pl.ds(off, K_TILE), :], False)
        return carry

    lax.fori_loop(0, n_steps, body, 0, unroll=True)
    outs = []
    for j in range(N_KV_HEADS):
        acc = acc_sc[j]
        out = acc[:, :HEAD_DIM] / acc[:, HEAD_DIM:HEAD_DIM + 1]
        outs += [out[r * tq:(r + 1) * tq] for r in range(Q_PER_KV)]
    o_ref[0] = jnp.concatenate(outs, axis=1).astype(BF16)


def _attention(q, k, v):
    bsz, _, tp, _ = q.shape
    n_steps = (tp - PREFIX) // K_TILE
    rows = Q_PER_KV * Q_TILE
    once = pl.Buffered(1)
    return pl.pallas_call(
        functools.partial(_attn_kernel, n_steps),
        grid=(bsz, tp // Q_TILE),
        in_specs=[pl.BlockSpec((1, N_Q_HEADS, Q_TILE, HEAD_DIM), lambda b, i: (b, 0, i, 0)),
                  pl.BlockSpec((1, N_KV_HEADS, tp, HEAD_DIM), lambda b, i: (b, 0, 0, 0), pipeline_mode=once),
                  pl.BlockSpec((1, N_KV_HEADS, tp, LANES), lambda b, i: (b, 0, 0, 0), pipeline_mode=once)],
        out_specs=pl.BlockSpec((1, Q_TILE, ATTN_WIDTH), lambda b, i: (b, i, 0)),
        out_shape=jax.ShapeDtypeStruct((bsz, tp, ATTN_WIDTH), BF16),
        scratch_shapes=[pltpu.VMEM((N_KV_HEADS, rows, LANES), F32),
                        pltpu.VMEM((N_KV_HEADS, rows, LANES), F32)],
        compiler_params=_cparams(("parallel", "arbitrary")),
    )(q, k, v)


def _ssm_local_kernel(n_chunks, u_ref, in_ref, rt_ref, sloc_ref):
    wl = rt_ref.shape[2]
    col = pl.program_id(1) * wl + lax.broadcasted_iota(jnp.int32, (1, wl), 1)
    valid = (col >= PAD_CHUNKS) & (col < n_chunks)
    groups_per_block = LANES // SSM_GROUP
    for t in range(CHUNK):
        for cb in range(SSM_WIDTH // LANES):
            piece = u_ref[0, cb, pl.ds(t, wl, stride=CHUNK), :]
            piece = jnp.where(valid, piece.T, 0.0).astype(BF16)
            for gi in range(groups_per_block):
                g = cb * groups_per_block + gi
                rt_ref[0, g * CHUNK_W + t * SSM_GROUP:g * CHUNK_W + (t + 1) * SSM_GROUP, :] = (
                    piece[gi * SSM_GROUP:(gi + 1) * SSM_GROUP, :])
    ns = SSM_STATE
    for m in range(N_SSM_GROUPS // 2):
        sl = [jnp.dot(in_ref[g], rt_ref[0, g * CHUNK_W:(g + 1) * CHUNK_W, :], preferred_element_type=F32)
              for g in (2 * m, 2 * m + 1)]
        for d in range(2):
            pair = jnp.concatenate([sl[0][2 * d * ns:(2 * d + 1) * ns], sl[1][2 * d * ns:(2 * d + 1) * ns],
                                    sl[0][(2 * d + 1) * ns:(2 * d + 2) * ns],
                                    sl[1][(2 * d + 1) * ns:(2 * d + 2) * ns]], axis=0).T
            sloc_ref[0, 2 * d, :, m * LANES:(m + 1) * LANES] = pair[:, :LANES]
            sloc_ref[0, 2 * d + 1, :, m * LANES:(m + 1) * LANES] = pair[:, LANES:]


def _ssm_local(u4, in_t, ncp):
    bsz, _, tp, _ = u4.shape
    n_chunks = tp // CHUNK
    wl = CHUNK_LANES
    return pl.pallas_call(
        functools.partial(_ssm_local_kernel, n_chunks),
        grid=(bsz, ncp // wl),
        in_specs=[pl.BlockSpec((1, SSM_WIDTH // LANES, wl * CHUNK, LANES), lambda b, c: (b, 0, c, 0)),
                  pl.BlockSpec(in_t.shape, lambda b, c: (0, 0, 0))],
        out_specs=[pl.BlockSpec((1, N_SSM_GROUPS * CHUNK_W, wl), lambda b, c: (b, 0, c)),
                   pl.BlockSpec((1, 4, wl, N_STATE_ROWS), lambda b, c: (b, 0, c, 0))],
        out_shape=[jax.ShapeDtypeStruct((bsz, N_SSM_GROUPS * CHUNK_W, ncp), BF16),
                   jax.ShapeDtypeStruct((bsz, 4, ncp, N_STATE_ROWS), F32)],
        compiler_params=_cparams(("parallel", "parallel")),
    )(u4, in_t)


def _ssm_scan_kernel(sloc_ref, tab_ref, car_ref, st_sc):
    ncp, lb = sloc_ref.shape[2], sloc_ref.shape[3]
    n_groups = ncp // SUBLANES
    row = lax.broadcasted_iota(jnp.int32, (ncp, 1), 0)
    sub = row & (SUBLANES - 1)

    for c0, reverse in ((0, False), (2, True)):
        re, im = sloc_ref[0, c0], sloc_ref[0, c0 + 1]
        for s in range(3):
            d = 1 << s
            keep = (sub < SUBLANES - d) if reverse else (sub >= d)
            shift = ncp - d if reverse else d
            r_re = jnp.where(keep, pltpu.roll(re, shift, 0), 0.0)
            r_im = jnp.where(keep, pltpu.roll(im, shift, 0), 0.0)
            a_re = tab_ref[c0, SUBLANES + s:SUBLANES + s + 1, :]
            a_im = tab_ref[c0 + 1, SUBLANES + s:SUBLANES + s + 1, :]
            re, im = re + (a_re * r_re - a_im * r_im), im + (a_re * r_im + a_im * r_re)
        st_sc[0] = re
        st_sc[1] = im
        p_re, p_im = tab_ref[c0, :SUBLANES, :], tab_ref[c0 + 1, :SUBLANES, :]
        edge = 0 if reverse else SUBLANES - 1

        def body(i, carry):
            c_re, c_im = carry
            grp = n_groups - 1 - i if reverse else i
            off = pl.multiple_of(grp * SUBLANES, SUBLANES)
            x_re = st_sc[0, pl.ds(off, SUBLANES), :] + (p_re * c_re - p_im * c_im)
            x_im = st_sc[1, pl.ds(off, SUBLANES), :] + (p_re * c_im + p_im * c_re)
            st_sc[0, pl.ds(off, SUBLANES), :] = x_re
            st_sc[1, pl.ds(off, SUBLANES), :] = x_im
            return x_re[edge:edge + 1], x_im[edge:edge + 1]

        zero = jnp.zeros((1, lb), F32)
        lax.fori_loop(0, n_groups, body, (zero, zero))
        keep = (row < ncp - 1) if reverse else (row >= 1)
        shift = ncp - 1 if reverse else 1
        car_ref[0, c0] = jnp.where(keep, pltpu.roll(st_sc[0], shift, 0), 0.0).astype(BF16)
        car_ref[0, c0 + 1] = jnp.where(keep, pltpu.roll(st_sc[1], shift, 0), 0.0).astype(BF16)


def _ssm_scan(sloc, tab, lb):
    bsz, _, ncp, _ = sloc.shape
    return pl.pallas_call(
        _ssm_scan_kernel,
        grid=(bsz, N_STATE_ROWS // lb),
        in_specs=[pl.BlockSpec((1, 4, ncp, lb), lambda b, r: (b, 0, 0, r)),
                  pl.BlockSpec((4, 2 * SUBLANES, lb), lambda b, r: (0, 0, r))],
        out_specs=pl.BlockSpec((1, 4, ncp, lb), lambda b, r: (b, 0, 0, r)),
        out_shape=jax.ShapeDtypeStruct((bsz, 4, ncp, N_STATE_ROWS), BF16),
        scratch_shapes=[pltpu.VMEM((2, ncp, lb), F32)],
        compiler_params=_cparams(("parallel", "parallel")),
    )(sloc, tab)


def _ssm_out_kernel(rt_ref, car_ref, m_ref, o_ref, wg_ref, bg_ref, out_ref, y_sc):
    wl = rt_ref.shape[2]
    nt = (((1,), (1,)), ((), ()))
    for m in range(N_SSM_GROUPS // 2):
        sc = jnp.concatenate([car_ref[0, c, :, m * LANES:(m + 1) * LANES] for c in range(4)], axis=1)
        y_car = lax.dot_general(o_ref[m], sc, nt, preferred_element_type=F32)
        for i in range(2):
            g = 2 * m + i
            rg = rt_ref[0, g * CHUNK_W:(g + 1) * CHUNK_W, :]
            y = jnp.dot(m_ref[g], rg, preferred_element_type=F32) + y_car[i * CHUNK_W:(i + 1) * CHUNK_W]
            y = jax.nn.gelu(y).astype(y_sc.dtype)
            for t in range(CHUNK):
                y_sc[t * SSM_WIDTH + g * SSM_GROUP:t * SSM_WIDTH + (g + 1) * SSM_GROUP, :] = (
                    y[t * SSM_GROUP:(t + 1) * SSM_GROUP, :])
    for t in range(CHUNK):
        slab = y_sc[t * SSM_WIDTH:(t + 1) * SSM_WIDTH, :]
        z = jnp.dot(wg_ref[...], slab, preferred_element_type=F32) + bg_ref[...]
        o = slab.astype(F32) * jax.nn.sigmoid(z)
        o_t = o.T
        for cb in range(SSM_WIDTH // LANES):
            out_ref[0, cb, pl.ds(t, wl, stride=CHUNK), :] = o_t[:, cb * LANES:(cb + 1) * LANES]


def _ssm_out(rt, car, m_t, o_t, wglu_t, bglu, tp):
    bsz, _, ncp = rt.shape
    wl = CHUNK_LANES
    once = pl.Buffered(1)
    return pl.pallas_call(
        _ssm_out_kernel,
        grid=(bsz, ncp // wl),
        in_specs=[pl.BlockSpec((1, N_SSM_GROUPS * CHUNK_W, wl), lambda b, c: (b, 0, c)),
                  pl.BlockSpec((1, 4, wl, N_STATE_ROWS), lambda b, c: (b, 0, c, 0)),
                  pl.BlockSpec(m_t.shape, lambda b, c: (0, 0, 0), pipeline_mode=once),
                  pl.BlockSpec(o_t.shape, lambda b, c: (0, 0, 0), pipeline_mode=once),
                  pl.BlockSpec(wglu_t.shape, lambda b, c: (0, 0)),
                  pl.BlockSpec(bglu.shape, lambda b, c: (0, 0))],
        out_specs=pl.BlockSpec((1, SSM_WIDTH // LANES, wl * CHUNK, LANES), lambda b, c: (b, 0, c, 0)),
        out_shape=jax.ShapeDtypeStruct((bsz, SSM_WIDTH // LANES, tp, LANES), F32),
        scratch_shapes=[pltpu.VMEM((CHUNK * SSM_WIDTH, wl), BF16)],
        compiler_params=_cparams(("parallel", "parallel"), SSM_OUT_VMEM_LIMIT),
    )(rt, car, m_t, o_t, wglu_t, bglu)


def _ssm_operators(lam_re, lam_im, log_dt, b_re, b_im, c_re, c_im, d_skip):
    g_n, p_n, h_n = N_SSM_GROUPS, SSM_STATE, SSM_GROUP
    hi = lax.Precision.HIGHEST
    kern, w_in, w_out, tab = [], [], [], []
    fwd_pow = jnp.arange(1, SUBLANES + 1, dtype=F32)
    level1_pow = jnp.array([1.0, 2.0, 4.0] + [0.0] * (SUBLANES - 3), F32)
    for direction in range(2):
        l_re, l_im = lam_re[direction].astype(F32), lam_im[direction].astype(F32)
        dt = jnp.exp(log_dt[direction].astype(F32))[:, None]
        ldt_re, ldt_im = l_re * dt, l_im * dt
        j = jnp.arange(CHUNK + 1, dtype=F32)[:, None, None]
        mag = jnp.exp(j * ldt_re[None])
        p_re, p_im = mag * jnp.cos(j * ldt_im[None]), mag * jnp.sin(j * ldt_im[None])
        a, b = p_re[1] - 1.0, p_im[1]
        den = l_re * l_re + l_im * l_im
        z_re, z_im = (a * l_re + b * l_im) / den, (b * l_re - a * l_im) / den
        br, bi = b_re[direction].astype(F32), b_im[direction].astype(F32)
        bb_re = z_re[:, :, None] * br - z_im[:, :, None] * bi
        bb_im = z_re[:, :, None] * bi + z_im[:, :, None] * br
        w_re = p_re[:, :, :, None] * bb_re[None] - p_im[:, :, :, None] * bb_im[None]
        w_im = p_re[:, :, :, None] * bb_im[None] + p_im[:, :, :, None] * bb_re[None]
        cr, ci = c_re[direction].astype(F32), c_im[direction].astype(F32)
        kern.append(jnp.einsum('ghp,jgpk->jghk', cr, w_re[:CHUNK], precision=hi)
                    - jnp.einsum('ghp,jgpk->jghk', ci, w_im[:CHUNK], precision=hi))
        w_in.append((w_re, w_im))
        w_out.append((cr[None] * p_re[:, :, None, :] - ci[None] * p_im[:, :, None, :],
                      cr[None] * p_im[:, :, None, :] + ci[None] * p_re[:, :, None, :]))
        scan_pow = CHUNK * jnp.concatenate([fwd_pow if direction == 0 else fwd_pow[::-1], level1_pow])
        d_mag = jnp.exp(scan_pow[:, None] * ldt_re.reshape(1, -1))
        d_ang = scan_pow[:, None] * ldt_im.reshape(1, -1)
        tab += [d_mag * jnp.cos(d_ang), d_mag * jnp.sin(d_ang)]
    t_idx = jnp.arange(CHUNK)
    diff = t_idx[:, None] - t_idx[None, :]
    kf = jnp.where((diff >= 0)[:, :, None, None, None], kern[0][jnp.clip(diff, 0, CHUNK - 1)], 0.0)
    kb = jnp.where((diff <= 0)[:, :, None, None, None], kern[1][jnp.clip(-diff, 0, CHUNK - 1)], 0.0)
    skip = (jnp.eye(CHUNK, dtype=F32)[:, :, None, None, None]
            * (jnp.eye(h_n, dtype=F32)[None] * d_skip.astype(F32).reshape(g_n, 1, h_n))[None, None])
    m_t = (kf + kb + skip).transpose(2, 0, 3, 1, 4).reshape(g_n, CHUNK_W, CHUNK_W)
    in_parts = []
    for direction in range(2):
        for part in w_in[direction]:
            sel = part[:CHUNK][::-1] if direction == 0 else part[:CHUNK]
            in_parts.append(sel.transpose(1, 2, 0, 3).reshape(g_n, p_n, CHUNK_W))
    in_t = jnp.concatenate(in_parts, axis=1)
    out_parts = []
    for direction in range(2):
        for sign, part in zip((1.0, -1.0), w_out[direction]):
            sel = part[1:] if direction == 0 else part[1:][::-1]
            out_parts.append(sign * sel.transpose(1, 0, 2, 3).reshape(g_n, CHUNK_W, p_n))
    eye2 = jnp.eye(2, dtype=F32)
    o_pair = jnp.stack([part.reshape(g_n // 2, 2, CHUNK_W, p_n) for part in out_parts], axis=3)
    o_pair = (o_pair[:, :, :, :, None, :] * eye2[None, :, None, None, :, None]).reshape(
        g_n // 2, 2 * CHUNK_W, 4 * 2 * p_n)
    return m_t.astype(BF16), in_t.astype(BF16), o_pair.astype(BF16), jnp.stack(tab)


def _outproj_kernel(attn_ref, ssm_ref, h_ref, wo_ref, ga_ref, gs_ref, lg_ref, lb_ref,
                    wr_ref, br_ref, x1_ref, gate_ref):
    a = attn_ref[0].astype(F32)
    s = jnp.concatenate([ssm_ref[0, cb] for cb in range(SSM_WIDTH // LANES)], axis=1)
    an = a * lax.rsqrt(jnp.mean(a * a, axis=-1, keepdims=True) + NORM_EPS) * ga_ref[...]
    sn = s * lax.rsqrt(jnp.mean(s * s, axis=-1, keepdims=True) + NORM_EPS) * gs_ref[...]
    mixed = (jnp.dot(an.astype(BF16), wo_ref[:ATTN_WIDTH, :], preferred_element_type=F32)
             + jnp.dot(sn.astype(BF16), wo_ref[ATTN_WIDTH:, :], preferred_element_type=F32))
    x1 = _layer_norm(DEEPNORM_ALPHA * h_ref[0] + mixed, lg_ref[...], lb_ref[...])
    x1_ref[0] = x1

    logits = jnp.dot(x1.astype(BF16), wr_ref[...], preferred_element_type=F32) + br_ref[...]
    lane = lax.broadcasted_iota(jnp.int32, logits.shape, 1)
    neg = -jnp.inf
    is_group = (lane >= N_EXPERTS) & (lane < N_EXPERTS + N_EXPERT_GROUPS)
    glog = jnp.where(is_group, logits, neg)
    gmax = jnp.max(glog, axis=1, keepdims=True)
    g_val = 1.0 / jnp.sum(jnp.exp(glog - gmax), axis=1, keepdims=True)
    g_idx = jnp.min(jnp.where(glog == gmax, lane, LANES), axis=1, keepdims=True) - N_EXPERTS
    in_group = (lane < N_EXPERTS) & ((lane >> 2) == g_idx)
    el = jnp.where(in_group, logits, neg)
    v1 = jnp.max(el, axis=1, keepdims=True)
    i1 = jnp.min(jnp.where(el == v1, lane, LANES), axis=1, keepdims=True)
    el2 = jnp.where(lane == i1, neg, el)
    v2 = jnp.max(el2, axis=1, keepdims=True)
    i2 = jnp.min(jnp.where(el2 == v2, lane, LANES), axis=1, keepdims=True)
    e2 = jnp.exp(v2 - v1)
    w1 = g_val / (1.0 + e2)
    gate_ref[0] = jnp.where(lane == i1, w1, jnp.where(lane == i2, w1 * e2, 0.0))


def _outproj(attn, ssm, h, wo, ga, gs, lg, lb, wr, br, tm):
    bsz, tp, _ = h.shape
    row = lambda b, j: (b, j, 0)
    c2 = lambda b, j: (0, 0)
    return pl.pallas_call(
        _outproj_kernel,
        grid=(bsz, tp // tm),
        in_specs=[pl.BlockSpec((1, tm, ATTN_WIDTH), row),
                  pl.BlockSpec((1, SSM_WIDTH // LANES, tm, LANES), lambda b, j: (b, 0, j, 0)),
                  pl.BlockSpec((1, tm, D_MODEL), row), pl.BlockSpec(wo.shape, c2),
                  pl.BlockSpec(ga.shape, c2), pl.BlockSpec(gs.shape, c2),
                  pl.BlockSpec(lg.shape, c2), pl.BlockSpec(lb.shape, c2),
                  pl.BlockSpec(wr.shape, c2), pl.BlockSpec(br.shape, c2)],
        out_specs=[pl.BlockSpec((1, tm, D_MODEL), row), pl.BlockSpec((1, tm, LANES), row)],
        out_shape=[jax.ShapeDtypeStruct((bsz, tp, D_MODEL), F32),
                   jax.ShapeDtypeStruct((bsz, tp, LANES), F32)],
        compiler_params=_cparams(("parallel", "parallel")),
    )(attn, ssm, h, wo, ga, gs, lg, lb, wr, br)


def _moe_kernel(x_ref, gate_ref, wgu_ref, wd_ref, lg_ref, lb_ref, y_ref, hh_sc):
    x = x_ref[0]
    xb = x.astype(BF16)
    gates = gate_ref[0]
    for e in range(N_EXPERTS):
        hgu = jnp.dot(xb, wgu_ref[e], preferred_element_type=F32)
        hh = jax.nn.silu(hgu[:, :EXPERT_FF]) * hgu[:, EXPERT_FF:] * gates[:, e:e + 1]
        hh_sc[:, e * EXPERT_FF:(e + 1) * EXPERT_FF] = hh.astype(BF16)
    moe = jnp.dot(hh_sc[...], wd_ref[...], preferred_element_type=F32)
    y_ref[0] = _layer_norm(DEEPNORM_ALPHA * x + moe, lg_ref[...], lb_ref[...])


def _moe(x1, gates, wgu, wd, lg, lb, tm):
    bsz, tp, _ = x1.shape
    row = lambda b, j: (b, j, 0)
    c2 = lambda b, j: (0, 0)
    once = pl.Buffered(1)
    return pl.pallas_call(
        _moe_kernel,
        grid=(bsz, tp // tm),
        in_specs=[pl.BlockSpec((1, tm, D_MODEL), row), pl.BlockSpec((1, tm, LANES), row),
                  pl.BlockSpec(wgu.shape, lambda b, j: (0, 0, 0), pipeline_mode=once),
                  pl.BlockSpec(wd.shape, c2, pipeline_mode=once),
                  pl.BlockSpec(lg.shape, c2), pl.BlockSpec(lb.shape, c2)],
        out_specs=pl.BlockSpec((1, tm, D_MODEL), row),
        out_shape=jax.ShapeDtypeStruct((bsz, tp, D_MODEL), F32),
        scratch_shapes=[pltpu.VMEM((tm, N_EXPERTS * EXPERT_FF), BF16)],
        compiler_params=_cparams(("parallel", "parallel")),
    )(x1, gates, wgu, wd, lg, lb)


def _rope_tables(n_real):
    r = jnp.arange(n_real, dtype=jnp.int32)
    row = jnp.concatenate([jnp.zeros((PREFIX,), F32), (r // GRID_W).astype(F32)])
    col = jnp.concatenate([jnp.zeros((PREFIX,), F32), (r % GRID_W).astype(F32)])
    n_freq = HEAD_DIM // 4
    inv_freq = ROPE_THETA ** (-jnp.arange(n_freq, dtype=F32) / n_freq)
    ang = jnp.concatenate([row[:, None] * inv_freq, col[:, None] * inv_freq], axis=-1)
    cos = jnp.repeat(jnp.cos(ang), 2, axis=-1)
    sin = jnp.repeat(jnp.sin(ang), 2, axis=-1) * jnp.tile(jnp.array([-1.0, 1.0], F32), HEAD_DIM // 2)
    return jnp.tile(cos, (1, 2)), jnp.tile(sin, (1, 2))


def _row_tile(tp, cap):
    best = 8
    for t in range(8, cap + 1, 8):
        if tp % t == 0:
            best = t
    return best


def _layer_weights(p, l):
    w_in = p['w_in'][l]
    m_t, in_t, o_t, tab = _ssm_operators(p['ssm_lambda_re'][l], p['ssm_lambda_im'][l], p['ssm_log_dt'][l],
                                         p['ssm_b_re'][l], p['ssm_b_im'][l], p['ssm_c_re'][l],
                                         p['ssm_c_im'][l], p['ssm_d'][l])
    head_avg = jnp.kron(jnp.eye(N_Q_HEADS, dtype=F32), jnp.full((HEAD_DIM, HEAD_DIM), 1.0 / HEAD_DIM, F32))
    wr = jnp.zeros((D_MODEL, LANES), F32)
    wr = wr.at[:, :N_EXPERTS].set(p['w_router'][l]).at[:, N_EXPERTS:N_EXPERTS + N_EXPERT_GROUPS].set(p['w_group'][l])
    br = jnp.zeros((1, LANES), F32)
    br = br.at[0, :N_EXPERTS].set(p['b_router'][l]).at[0, N_EXPERTS:N_EXPERTS + N_EXPERT_GROUPS].set(p['b_group'][l])
    return dict(
        w_in=w_in.astype(BF16),
        gq=(jnp.tile(p['q_norm_g'][l], N_Q_HEADS) * (HEAD_DIM ** -0.5)).reshape(1, ATTN_WIDTH),
        gk=jnp.tile(p['k_norm_g'][l], N_KV_HEADS).reshape(1, KV_WIDTH),
        bdq=head_avg.astype(BF16), bdk=head_avg[:KV_WIDTH, :KV_WIDTH].astype(BF16),
        m_t=m_t, in_t=in_t, o_t=o_t, tab=tab,
        wglu_t=p['w_glu'][l].T.astype(BF16), bglu=p['b_glu'][l].reshape(SSM_WIDTH, 1).astype(F32),
        wo=p['w_out'][l].astype(BF16),
        ga=p['attn_out_g'][l].reshape(1, -1), gs=p['ssm_out_g'][l].reshape(1, -1),
        ln1g=p['ln1_g'][l].reshape(1, -1), ln1b=p['ln1_b'][l].reshape(1, -1),
        wr=wr.astype(BF16), br=br,
        wgu=jnp.concatenate([p['w_gate'][l], p['w_up'][l]], axis=-1).astype(BF16),
        wd=p['w_down'][l].reshape(N_EXPERTS * EXPERT_FF, D_MODEL).astype(BF16),
        ln2g=p['ln2_g'][l].reshape(1, -1), ln2b=p['ln2_b'][l].reshape(1, -1),
    )


def _trunk(x, p, weights):
    bsz, n_real, _ = x.shape
    tp = n_real + PREFIX
    meta = jnp.broadcast_to(p['meta_tokens'][None].astype(x.dtype), (bsz, N_META, D_MODEL))
    h = jnp.concatenate([jnp.zeros((bsz, N_PAD, D_MODEL), x.dtype), meta, x], axis=1)
    cos_t, sin_t = _rope_tables(n_real)
    n_chunks = tp // CHUNK
    ncp = -(-n_chunks // CHUNK_LANES) * CHUNK_LANES
    tm = _row_tile(tp, 1408)
    ln = (p['ln_in_g'].reshape(1, -1), p['ln_in_b'].reshape(1, -1))
    for l in range(DEPTH):
        w = weights[l]
        h, u4, q, k, v = _inproj(h, cos_t, sin_t, ln if l == 0 else None, w['w_in'], w['gq'], w['gk'],
                                 w['bdq'], w['bdk'], tm)
        attn = _attention(q, k, v)
        rt, sloc = _ssm_local(u4, w['in_t'], ncp)
        car = _ssm_scan(sloc, w['tab'], SCAN_LANES)
        ssm = _ssm_out(rt, car, w['m_t'], w['o_t'], w['wglu_t'], w['bglu'], tp)
        x1, gates = _outproj(attn, ssm, h, w['wo'], w['ga'], w['gs'], w['ln1g'], w['ln1b'],
                             w['wr'], w['br'], tm)
        h = _moe(x1, gates, w['wgu'], w['wd'], w['ln2g'], w['ln2b'], _row_tile(tp, MOE_ROWS))
    return h[:, PREFIX:]


def kernel(x_prompt, x_sample, meta_tokens, ln_in_g, ln_in_b, w_in, q_norm_g, k_norm_g, ssm_lambda_re, ssm_lambda_im, ssm_log_dt, ssm_b_re, ssm_b_im, ssm_c_re, ssm_c_im, ssm_d, w_glu, b_glu, attn_out_g, ssm_out_g, w_out, ln1_g, ln1_b, w_group, b_group, w_router, b_router, w_gate, w_up, w_down, ln2_g, ln2_b):
    p = dict(meta_tokens=meta_tokens, ln_in_g=ln_in_g, ln_in_b=ln_in_b, w_in=w_in,
             q_norm_g=q_norm_g, k_norm_g=k_norm_g, ssm_lambda_re=ssm_lambda_re,
             ssm_lambda_im=ssm_lambda_im, ssm_log_dt=ssm_log_dt, ssm_b_re=ssm_b_re,
             ssm_b_im=ssm_b_im, ssm_c_re=ssm_c_re, ssm_c_im=ssm_c_im, ssm_d=ssm_d,
             w_glu=w_glu, b_glu=b_glu, attn_out_g=attn_out_g, ssm_out_g=ssm_out_g,
             w_out=w_out, ln1_g=ln1_g, ln1_b=ln1_b, w_group=w_group, b_group=b_group,
             w_router=w_router, b_router=b_router, w_gate=w_gate, w_up=w_up,
             w_down=w_down, ln2_g=ln2_g, ln2_b=ln2_b)
    weights = [_layer_weights(p, l) for l in range(DEPTH)]
    return (_trunk(x_prompt, p, weights), _trunk(x_sample, p, weights))
```
